```python
import jax, jax.numpy as jnp
from jax import lax
import numpy as np

D_MODEL = 1024
BATCH = 4
SEQ = 8192
DEPTH = 2

D_FF = 2816
SSD_EXPAND = 2
SSD_D_INNER = SSD_EXPAND * D_MODEL
SSD_HEAD_DIM = 64
SSD_HEADS = SSD_D_INNER // SSD_HEAD_DIM
SSD_GROUPS = 4
SSD_HPG = SSD_HEADS // SSD_GROUPS
SSD_STATE = 128
SSD_CONV = 4
SSD_CHUNK = 128
SSD_CONV_DIM = SSD_D_INNER + 2 * SSD_GROUPS * SSD_STATE
MLA_HEADS = 8
MLA_Q_LORA = 512
MLA_KV_LORA = 256
MLA_NOPE = 128
MLA_ROPE = 64
MLA_V = 128
MLA_QK = MLA_NOPE + MLA_ROPE
ATTN_BLOCK = 128
ROPE_THETA = 10000.0
EPS = 1e-6
IN_SPLIT_SIZES = (SSD_D_INNER, SSD_CONV_DIM, SSD_HEADS, MLA_Q_LORA, MLA_KV_LORA, MLA_ROPE, 2 * D_MODEL)
D_IN_PROJ = SSD_D_INNER + SSD_CONV_DIM + SSD_HEADS + MLA_Q_LORA + MLA_KV_LORA + MLA_ROPE + 2 * D_MODEL

kernel_name = "macaron_gated_ssd_mla_hybrid"


def rms_norm(x, g):
    xf = x.astype(jnp.float32)
    y = xf * lax.rsqrt(jnp.mean(xf * xf, axis=-1, keepdims=True) + EPS)
    return (y * g.astype(jnp.float32)).astype(x.dtype)


def swiglu(x, w13, w2):
    gu = x @ w13
    gate, up = gu[..., :D_FF], gu[..., D_FF:]
    return (jax.nn.silu(gate) * up) @ w2


def split_cols(t, sizes):
    out, start = [], 0
    for n in sizes:
        out.append(t[..., start:start + n])
        start += n
    return out


def rope_tables(positions):
    inv = 1.0 / (ROPE_THETA ** (jnp.arange(0, MLA_ROPE, 2, dtype=jnp.float32) / MLA_ROPE))
    ang = positions.astype(jnp.float32)[..., None] * inv
    return jnp.cos(ang), jnp.sin(ang)


def apply_rope(t, cos, sin):
    half = MLA_ROPE // 2
    tf = t.astype(jnp.float32)
    t1, t2 = tf[..., :half], tf[..., half:]
    c, s = cos[:, :, None, :], sin[:, :, None, :]
    return jnp.concatenate([t1 * c - t2 * s, t2 * c + t1 * s], axis=-1).astype(t.dtype)


def causal_depthwise_conv(t, w, b):
    y = lax.conv_general_dilated(
        t, w[:, None, :].astype(t.dtype), window_strides=(1,), padding=[(SSD_CONV - 1, 0)],
        dimension_numbers=('NWC', 'WIO', 'NWC'), feature_group_count=t.shape[-1])
    return y + b.astype(t.dtype)


def ssd_chunked_scan(xh, dt, a, bmat, cmat):
    bsz, s = xh.shape[0], xh.shape[1]
    nc = s // SSD_CHUNK

    def chunks(t):
        t = t.reshape((bsz, nc, SSD_CHUNK) + t.shape[2:])
        return jnp.moveaxis(t, 1, 0)

    x_c = chunks(xh.astype(jnp.float32).reshape(bsz, s, SSD_GROUPS, SSD_HPG, SSD_HEAD_DIM))
    dt_c = chunks(dt.reshape(bsz, s, SSD_GROUPS, SSD_HPG))
    b_c = chunks(bmat.astype(jnp.float32))
    c_c = chunks(cmat.astype(jnp.float32))
    a_g = a.reshape(SSD_GROUPS, SSD_HPG)
    causal = jnp.tril(jnp.ones((SSD_CHUNK, SSD_CHUNK), dtype=bool))[None, :, :, None, None]

    def step(state, inp):
        xc, dtc, bc, cc = inp
        acum = jnp.cumsum(dtc * a_g, axis=1)
        seg = acum[:, :, None] - acum[:, None, :]
        decay = jnp.exp(jnp.where(causal, seg, -jnp.inf))
        cb = jnp.einsum('btgn,bsgn->btsg', cc, bc)
        xdt = xc * dtc[..., None]
        y_diag = jnp.einsum('btsg,btsgh,bsghp->btghp', cb, decay, xdt)
        y_off = jnp.einsum('btgn,bghpn->btghp', cc, state) * jnp.exp(acum)[..., None]
        last = acum[:, -1]
        w_state = jnp.exp(last[:, None] - acum)
        new_state = (state * jnp.exp(last)[..., None, None]
                     + jnp.einsum('bsgn,bsgh,bsghp->bghpn', bc, w_state, xdt))
        return new_state, y_diag + y_off

    state0 = jnp.zeros((bsz, SSD_GROUPS, SSD_HPG, SSD_HEAD_DIM, SSD_STATE), jnp.float32)
    _, y = lax.scan(step, state0, (x_c, dt_c, b_c, c_c))
    return jnp.moveaxis(y, 0, 1).reshape(bsz, s, SSD_HEADS, SSD_HEAD_DIM)


def ssd_branch(z, xbc, dt_raw, conv_w, conv_b, dt_bias, a_log, d_skip, norm_g, w_out):
    bsz, s, _ = z.shape
    xbc = jax.nn.silu(causal_depthwise_conv(xbc, conv_w, conv_b))
    xs, bm, cm = split_cols(xbc, (SSD_D_INNER, SSD_GROUPS * SSD_STATE, SSD_GROUPS * SSD_STATE))
    xh = xs.reshape(bsz, s, SSD_HEADS, SSD_HEAD_DIM)
    bm = bm.reshape(bsz, s, SSD_GROUPS, SSD_STATE)
    cm = cm.reshape(bsz, s, SSD_GROUPS, SSD_STATE)
    dt = jax.nn.softplus(dt_raw.astype(jnp.float32) + dt_bias.astype(jnp.float32))
    a = -jnp.exp(a_log.astype(jnp.float32))
    y = ssd_chunked_scan(xh, dt, a, bm, cm) + xh.astype(jnp.float32) * d_skip.astype(jnp.float32)[:, None]
    y = y.reshape(bsz, s, SSD_D_INNER).astype(z.dtype)
    y = rms_norm(y * jax.nn.silu(z), norm_g)
    return y @ w_out


def causal_block_attention(q, k, v):
    bsz, s, h, dq = q.shape
    nb = s // ATTN_BLOCK
    scale = dq ** -0.5
    qb = jnp.moveaxis(q.reshape(bsz, nb, ATTN_BLOCK, h, dq), 1, 0)
    kpos = jnp.arange(s)

    def one_block(args):
        i, qi = args
        sc = jnp.einsum('bqhd,bkhd->bhqk', qi, k, preferred_element_type=jnp.float32) * scale
        qpos = i * ATTN_BLOCK + jnp.arange(ATTN_BLOCK)
        sc = jnp.where(kpos[None, :] <= qpos[:, None], sc, -jnp.inf)
        p = jax.nn.softmax(sc, axis=-1)
        return jnp.einsum('bhqk,bkhd->bqhd', p.astype(v.dtype), v)

    o = lax.map(one_block, (jnp.arange(nb), qb))
    return jnp.moveaxis(o, 0, 1).reshape(bsz, s, h, v.shape[-1])


def mla_branch(cq, ckv, kr, cos, sin, q_lora_g, w_uq, kv_lora_g, w_ukv, q_norm_g, k_norm_g, w_out):
    bsz, s, _ = cq.shape
    q = (rms_norm(cq, q_lora_g) @ w_uq).reshape(bsz, s, MLA_HEADS, MLA_QK)
    kv = (rms_norm(ckv, kv_lora_g) @ w_ukv).reshape(bsz, s, MLA_HEADS, MLA_NOPE + MLA_V)
    k_nope, v = kv[..., :MLA_NOPE], kv[..., MLA_NOPE:]
    k_pe = jnp.broadcast_to(kr[:, :, None, :], (bsz, s, MLA_HEADS, MLA_ROPE))
    k = jnp.concatenate([k_nope, k_pe], axis=-1)
    q = rms_norm(q, q_norm_g)
    k = rms_norm(k, k_norm_g)
    q = jnp.concatenate([q[..., :MLA_NOPE], apply_rope(q[..., MLA_NOPE:], cos, sin)], axis=-1)
    k = jnp.concatenate([k[..., :MLA_NOPE], apply_rope(k[..., MLA_NOPE:], cos, sin)], axis=-1)
    o = causal_block_attention(q, k, v)
    return o.reshape(bsz, s, MLA_HEADS * MLA_V) @ w_out


def setup_inputs(seed: int = 0) -> dict:
    key = jax.random.key(seed)
    ks = jax.random.split(key, 32)
    f32 = jnp.float32

    def nrm(k, shape, fan_in):
        return jax.random.normal(k, shape, f32) * (fan_in ** -0.5)

    def gain(k, n):
        return 1.0 + 0.02 * jax.random.normal(k, (DEPTH, n), f32)

    dt0 = jnp.exp(jax.random.uniform(ks[10], (DEPTH, SSD_HEADS), f32) * (np.log(0.1) - np.log(0.001)) + np.log(0.001))
    dt_bias = dt0 + jnp.log(-jnp.expm1(-dt0))
    a_log = jnp.log(jax.random.uniform(ks[11], (DEPTH, SSD_HEADS), f32, 1.0, 16.0))
    return {
        "x": jax.random.normal(ks[0], (BATCH, SEQ, D_MODEL), f32),
        "positions": jnp.broadcast_to(jnp.arange(SEQ, dtype=jnp.int32)[None, :], (BATCH, SEQ)),
        "ln_ffn1": gain(ks[1], D_MODEL),
        "ffn1_w13": nrm(ks[2], (DEPTH, D_MODEL, 2 * D_FF), D_MODEL),
        "ffn1_w2": nrm(ks[3], (DEPTH, D_FF, D_MODEL), D_FF),
        "ln_mix": gain(ks[4], D_MODEL),
        "w_in": nrm(ks[5], (DEPTH, D_MODEL, D_IN_PROJ), D_MODEL),
        "conv_w": nrm(ks[6], (DEPTH, SSD_CONV, SSD_CONV_DIM), SSD_CONV),
        "conv_b": 0.02 * jax.random.normal(ks[7], (DEPTH, SSD_CONV_DIM), f32),
        "dt_bias": dt_bias,
        "a_log": a_log,
        "d_skip": 1.0 + 0.1 * jax.random.normal(ks[12], (DEPTH, SSD_HEADS), f32),
        "ssd_norm": gain(ks[13], SSD_D_INNER),
        "w_ssd_out": nrm(ks[14], (DEPTH, SSD_D_INNER, D_MODEL), SSD_D_INNER),
        "q_lora_norm": gain(ks[15], MLA_Q_LORA),
        "w_uq": nrm(ks[16], (DEPTH, MLA_Q_LORA, MLA_HEADS * MLA_QK), MLA_Q_LORA),
        "kv_lora_norm": gain(ks[17], MLA_KV_LORA),
        "w_ukv": nrm(ks[18], (DEPTH, MLA_KV_LORA, MLA_HEADS * (MLA_NOPE + MLA_V)), MLA_KV_LORA),
        "q_norm": gain(ks[19], MLA_QK),
        "k_norm": gain(ks[20], MLA_QK),
        "w_mla_out": nrm(ks[21], (DEPTH, MLA_HEADS * MLA_V, D_MODEL), MLA_HEADS * MLA_V),
        "w_o": nrm(ks[22], (DEPTH, D_MODEL, D_MODEL), D_MODEL),
        "ln_ffn2": gain(ks[23], D_MODEL),
        "ffn2_w13": nrm(ks[24], (DEPTH, D_MODEL, 2 * D_FF), D_MODEL),
        "ffn2_w2": nrm(ks[25], (DEPTH, D_FF, D_MODEL), D_FF),
    }


def reference(x, positions, ln_ffn1, ffn1_w13, ffn1_w2, ln_mix, w_in, conv_w, conv_b, dt_bias,
              a_log, d_skip, ssd_norm, w_ssd_out, q_lora_norm, w_uq, kv_lora_norm, w_ukv,
              q_norm, k_norm, w_mla_out, w_o, ln_ffn2, ffn2_w13, ffn2_w2):
    cos, sin = rope_tables(positions)
    h = x
    for l in range(DEPTH):
        h = h + 0.5 * swiglu(rms_norm(h, ln_ffn1[l]), ffn1_w13[l], ffn1_w2[l])
        u = rms_norm(h, ln_mix[l])
        z, xbc, dt_raw, cq, ckv, kr, gates = split_cols(u @ w_in[l], IN_SPLIT_SIZES)
        y_ssd = ssd_branch(z, xbc, dt_raw, conv_w[l], conv_b[l], dt_bias[l], a_log[l], d_skip[l],
                           ssd_norm[l], w_ssd_out[l])
        y_mla = mla_branch(cq, ckv, kr, cos, sin, q_lora_norm[l], w_uq[l], kv_lora_norm[l], w_ukv[l],
                           q_norm[l], k_norm[l], w_mla_out[l])
        g = jax.nn.sigmoid(gates.astype(jnp.float32)).astype(h.dtype)
        merged = g[..., :D_MODEL] * y_ssd + g[..., D_MODEL:] * y_mla
        h = h + merged @ w_o[l]
        h = h + 0.5 * swiglu(rms_norm(h, ln_ffn2[l]), ffn2_w13[l], ffn2_w2[l])
    return h
```

```python
import functools

import jax
import jax.numpy as jnp
import numpy as np
from jax import lax
from jax.experimental import pallas as pl
from jax.experimental.pallas import tpu as pltpu

F32 = jnp.float32
BF16 = jnp.bfloat16

D_MODEL = 1024
D_FF = 2816
SSD_D_INNER = 2048
SSD_HEAD_DIM = 64
SSD_HEADS = 32
SSD_GROUPS = 4
SSD_HPG = 8
SSD_STATE = 128
SSD_CONV = 4
SSD_CHUNK = 128
SSD_BC = SSD_GROUPS * SSD_STATE
SSD_CONV_DIM = SSD_D_INNER + 2 * SSD_BC
MLA_HEADS = 8
MLA_Q_LORA = 512
MLA_KV_LORA = 256
MLA_NOPE = 128
MLA_ROPE = 64
MLA_V = 128
MLA_QK = MLA_NOPE + MLA_ROPE
MLA_QK_PAD = 256
ROPE_THETA = 10000.0
EPS = 1e-6

LANES = 128
VMEM_LIMIT_BYTES = 56 * 1024 * 1024

TOKEN_TILE = 256
SSD_STEP = 256
ATTN_BLOCK = 512
CONV_HALO = 8

NT_DIMS = (((1,), (1,)), ((), ()))
TN_DIMS = (((0,), (0,)), ((), ()))


def _params(n_grid_axes):
    return pltpu.CompilerParams(
        dimension_semantics=("arbitrary",) * n_grid_axes,
        vmem_limit_bytes=VMEM_LIMIT_BYTES,
    )


def _resident(shape):
    zeros = (0,) * len(shape)
    return pl.BlockSpec(shape, lambda *_: zeros, pipeline_mode=pl.Buffered(1))


def _rms(x, gain):
    return x * lax.rsqrt(jnp.mean(x * x, axis=-1, keepdims=True) + EPS) * gain


def _silu(x):
    return x * jax.nn.sigmoid(x)


def _ffn_kernel(h_ref, g_ref, w13_ref, w2_ref, o_ref):
    x = h_ref[...]
    xb = _rms(x, g_ref[...]).astype(BF16)
    gate = jnp.dot(xb, w13_ref[:, :D_FF], preferred_element_type=F32)
    up = jnp.dot(xb, w13_ref[:, D_FF:], preferred_element_type=F32)
    hidden = (_silu(gate) * up).astype(BF16)
    o_ref[...] = x + 0.5 * jnp.dot(hidden, w2_ref[...], preferred_element_type=F32)


def _ffn(h, gain, w13, w2):
    t = h.shape[0]
    return pl.pallas_call(
        _ffn_kernel,
        grid=(t // TOKEN_TILE,),
        in_specs=[
            pl.BlockSpec((TOKEN_TILE, D_MODEL), lambda i: (i, 0)),
            _resident((1, D_MODEL)),
            _resident((D_MODEL, 2 * D_FF)),
            _resident((D_FF, D_MODEL)),
        ],
        out_specs=pl.BlockSpec((TOKEN_TILE, D_MODEL), lambda i: (i, 0)),
        out_shape=jax.ShapeDtypeStruct((t, D_MODEL), F32),
        compiler_params=_params(1),
        name="ffn",
    )(h, gain, w13, w2)


def _in_proj_kernel(h_ref, g_ref, wz_ref, wxbc_ref, wdt_ref, wdtT_ref, wcq_ref, wckv_ref, wkrx_ref, wg_ref,
                    zs_ref, xbc_ref, dt_ref, dtT_ref, cq_ref, ckv_ref, krx_ref, gate_ref):
    xb = _rms(h_ref[...], g_ref[...]).astype(BF16)

    def proj(w_ref):
        return jnp.dot(xb, w_ref[...], preferred_element_type=F32)

    zs_ref[...] = _silu(proj(wz_ref)).astype(BF16)
    xbc_ref[...] = proj(wxbc_ref).astype(BF16)
    dt_ref[...] = proj(wdt_ref)
    dtT_ref[...] = lax.dot_general(wdtT_ref[...], xb, NT_DIMS, preferred_element_type=F32)
    cq_ref[...] = proj(wcq_ref).astype(BF16)
    ckv_ref[...] = proj(wckv_ref).astype(BF16)
    krx_ref[...] = proj(wkrx_ref)
    gate_ref[...] = jax.nn.sigmoid(proj(wg_ref)).astype(BF16)


def _in_proj(h, gain, w, bsz, seq):
    tm = TOKEN_TILE
    nt = seq // tm
    widths = (SSD_D_INNER, SSD_CONV_DIM, SSD_HEADS, None, MLA_Q_LORA, MLA_KV_LORA, LANES, 2 * D_MODEL)
    dtypes = (BF16, BF16, F32, F32, BF16, BF16, F32, BF16)
    out_specs, out_shape = [], []
    for width, dtype in zip(widths, dtypes):
        if width is None:
            out_specs.append(pl.BlockSpec((None, SSD_HEADS, tm), lambda b, i: (b, 0, i)))
            out_shape.append(jax.ShapeDtypeStruct((bsz, SSD_HEADS, seq), dtype))
        else:
            out_specs.append(pl.BlockSpec((None, tm, width), lambda b, i: (b, i, 0)))
            out_shape.append(jax.ShapeDtypeStruct((bsz, seq, width), dtype))
    weights = (w["w_z"], w["w_xbc"], w["w_dt"], w["w_dtT"], w["w_cq"], w["w_ckv"], w["w_krx"], w["w_gate"])
    return pl.pallas_call(
        _in_proj_kernel,
        grid=(bsz, nt),
        in_specs=[pl.BlockSpec((None, tm, D_MODEL), lambda b, i: (b, i, 0)), _resident((1, D_MODEL))]
        + [_resident(x.shape) for x in weights],
        out_specs=out_specs,
        out_shape=out_shape,
        compiler_params=_params(2),
        name="in_proj",
    )(h.reshape(bsz, seq, D_MODEL), gain, *weights)


def _mla_prep_kernel(cq_ref, ckv_ref, krx_ref, pos_ref, inv_ref, gql_ref, wuqT_ref, gkvl_ref, wuk_ref, wuvT_ref,
                     gq_ref, gkn_ref, gkr_ref, qT_ref, k_ref, vT_ref):
    tm = cq_ref.shape[0]
    half = MLA_ROPE // 2
    ang = inv_ref[...] * pos_ref[...].astype(F32)
    cos, sin = jnp.cos(ang), jnp.sin(ang)

    qn = _rms(cq_ref[...].astype(F32), gql_ref[...]).astype(BF16)
    qT = lax.dot_general(wuqT_ref[...], qn, NT_DIMS, preferred_element_type=F32)
    gq = jnp.broadcast_to(gq_ref[...], (MLA_QK, tm))
    for h in range(MLA_HEADS):
        blk = qT[h * MLA_QK:(h + 1) * MLA_QK]
        inv_rms = lax.rsqrt(jnp.sum(blk * blk, axis=0, keepdims=True) * (1.0 / MLA_QK) + EPS)
        xn = blk * inv_rms * gq
        t1 = xn[MLA_NOPE:MLA_NOPE + half]
        t2 = xn[MLA_NOPE + half:]
        qT_ref[h, 0:MLA_NOPE, :] = xn[:MLA_NOPE].astype(BF16)
        qT_ref[h, MLA_NOPE:MLA_NOPE + half, :] = (t1 * cos - t2 * sin).astype(BF16)
        qT_ref[h, MLA_NOPE + half:MLA_QK, :] = (t2 * cos + t1 * sin).astype(BF16)
        qT_ref[h, MLA_QK:, :] = jnp.zeros((MLA_QK_PAD - MLA_QK, tm), BF16)

    kvn = _rms(ckv_ref[...].astype(F32), gkvl_ref[...]).astype(BF16)
    kn = jnp.dot(kvn, wuk_ref[...], preferred_element_type=F32)
    vT = lax.dot_general(wuvT_ref[...], kvn, NT_DIMS, preferred_element_type=F32)
    vT_ref[...] = vT.reshape(MLA_HEADS, MLA_V, tm).astype(BF16)

    krx = krx_ref[...]
    rope_ss = 0.5 * jnp.sum(krx * krx, axis=-1, keepdims=True)
    table = (jnp.concatenate([cos, cos, sin, sin], axis=0) * gkr_ref[...]).T
    prod = krx * table
    k_rope = prod + pltpu.roll(prod, MLA_ROPE, 1)
    gkn = gkn_ref[...]
    for h in range(MLA_HEADS):
        knh = kn[:, h * MLA_NOPE:(h + 1) * MLA_NOPE]
        ss = jnp.sum(knh * knh, axis=-1, keepdims=True) + rope_ss
        inv_rms = lax.rsqrt(ss * (1.0 / MLA_QK) + EPS)
        k_ref[h, :, 0:MLA_NOPE] = (knh * inv_rms * gkn).astype(BF16)
        k_ref[h, :, MLA_NOPE:] = (k_rope * inv_rms).astype(BF16)


def _mla_prep(cq, ckv, krx, pos_row, w, bsz, seq):
    tm = ATTN_BLOCK
    nt = seq // tm
    consts = (w["rope_inv"], w["g_q_lora"], w["w_uqT"], w["g_kv_lora"], w["w_uk"], w["w_uvT"],
              w["g_q_col"], w["g_k_nope"], w["g_k_rope_col"])
    in_specs = [
        pl.BlockSpec((None, tm, MLA_Q_LORA), lambda b, i: (b, i, 0)),
        pl.BlockSpec((None, tm, MLA_KV_LORA), lambda b, i: (b, i, 0)),
        pl.BlockSpec((None, tm, LANES), lambda b, i: (b, i, 0)),
        pl.BlockSpec((None, 1, tm), lambda b, i: (b, 0, i)),
    ] + [_resident(x.shape) for x in consts]
    out_specs = [
        pl.BlockSpec((None, MLA_HEADS, MLA_QK_PAD, tm), lambda b, i: (b, 0, 0, i)),
        pl.BlockSpec((None, MLA_HEADS, tm, MLA_QK_PAD), lambda b, i: (b, 0, i, 0)),
        pl.BlockSpec((None, MLA_HEADS, None, MLA_V, tm), lambda b, i: (b, 0, i, 0, 0)),
    ]
    out_shape = [
        jax.ShapeDtypeStruct((bsz, MLA_HEADS, MLA_QK_PAD, seq), BF16),
        jax.ShapeDtypeStruct((bsz, MLA_HEADS, seq, MLA_QK_PAD), BF16),
        jax.ShapeDtypeStruct((bsz, MLA_HEADS, nt, MLA_V, tm), BF16),
    ]
    return pl.pallas_call(
        _mla_prep_kernel,
        grid=(bsz, nt),
        in_specs=in_specs,
        out_specs=out_specs,
        out_shape=out_shape,
        compiler_params=_params(2),
        name="mla_prep",
    )(cq, ckv, krx, pos_row, *consts)


def _flash_kernel(qT_ref, k_ref, vT_ref, o_ref, m_ref, l_ref, acc_ref):
    blk = ATTN_BLOCK
    i = pl.program_id(2)
    m_ref[...] = jnp.full(m_ref.shape, -jnp.inf, F32)
    l_ref[...] = jnp.zeros(l_ref.shape, F32)
    acc_ref[...] = jnp.zeros(acc_ref.shape, F32)
    qT = qT_ref[...]

    def step(j, on_diagonal):
        k_blk = k_ref[pl.ds(pl.multiple_of(j * blk, blk), blk), :]
        sT = jnp.dot(k_blk, qT, preferred_element_type=F32)
        if on_diagonal:
            kpos = lax.broadcasted_iota(jnp.int32, (blk, blk), 0)
            qpos = lax.broadcasted_iota(jnp.int32, (blk, blk), 1)
            sT = jnp.where(kpos <= qpos, sT, -jnp.inf)
        m_prev = m_ref[...]
        m_new = jnp.maximum(m_prev, jnp.max(sT, axis=0, keepdims=True))
        alpha = jnp.exp(m_prev - m_new)
        p = jnp.exp(sT - m_new)
        l_ref[...] = alpha * l_ref[...] + jnp.sum(p, axis=0, keepdims=True)
        pv = jnp.dot(vT_ref[j], p.astype(BF16), preferred_element_type=F32)
        acc_ref[...] = alpha * acc_ref[...] + pv
        m_ref[...] = m_new

    def below_diagonal(j, carry):
        step(j, False)
        return carry

    lax.fori_loop(0, i, below_diagonal, 0)
    step(i, True)
    o_ref[...] = (acc_ref[...] / l_ref[...]).T.astype(BF16)


def _flash(qT, k, vT, bsz, seq):
    blk = ATTN_BLOCK
    nb = seq // blk
    return pl.pallas_call(
        _flash_kernel,
        grid=(bsz, MLA_HEADS, nb),
        in_specs=[
            pl.BlockSpec((None, None, MLA_QK_PAD, blk), lambda b, h, i: (b, h, 0, i)),
            pl.BlockSpec((None, None, seq, MLA_QK_PAD), lambda b, h, i: (b, h, 0, 0)),
            pl.BlockSpec((None, None, nb, MLA_V, blk), lambda b, h, i: (b, h, 0, 0, 0)),
        ],
        out_specs=pl.BlockSpec((None, blk, MLA_V), lambda b, h, i: (b, i, h)),
        out_shape=jax.ShapeDtypeStruct((bsz, seq, MLA_HEADS * MLA_V), BF16),
        scratch_shapes=[
            pltpu.VMEM((1, blk), F32),
            pltpu.VMEM((1, blk), F32),
            pltpu.VMEM((MLA_V, blk), F32),
        ],
        compiler_params=_params(3),
        name="flash",
    )(qT, k, vT)


def _softplus(x):
    return jnp.maximum(x, 0.0) + jnp.log1p(jnp.exp(-jnp.abs(x)))


def _expand_heads(v):
    rows = v.shape[0]
    lane = lax.broadcasted_iota(jnp.int32, (rows, LANES), 1)
    parts = []
    for j in range(SSD_HEADS // 2):
        parts.append(jnp.where(lane < SSD_HEAD_DIM, v[:, 2 * j:2 * j + 1], v[:, 2 * j + 1:2 * j + 2]))
    return jnp.concatenate(parts, axis=1)


def _ssd_kernel(xbc_ref, dt_ref, dtT_ref, convw_ref, convb_ref, dtb_ref, dtbT_ref, alog_ref, alogT_ref, dskip_ref,
                y_ref, state_ref, ext_ref):
    step_len = xbc_ref.shape[0]
    cl = SSD_CHUNK

    @pl.when(pl.program_id(1) == 0)
    def _():
        state_ref[...] = jnp.zeros(state_ref.shape, F32)
        ext_ref[0:CONV_HALO, :] = jnp.zeros((CONV_HALO, SSD_CONV_DIM), F32)

    ext_ref[CONV_HALO:, :] = xbc_ref[...].astype(F32)
    conv = convb_ref[...]
    for k in range(SSD_CONV):
        start = CONV_HALO - (SSD_CONV - 1) + k
        conv = conv + convw_ref[k:k + 1, :] * ext_ref[start:start + step_len, :]
    ext_ref[0:CONV_HALO, :] = ext_ref[step_len:step_len + CONV_HALO, :]
    xc = _silu(conv)

    dt_all = _softplus(dt_ref[...] + dtb_ref[...])
    dtT_all = _softplus(dtT_ref[...] + dtbT_ref[...])
    a_row = -jnp.exp(alog_ref[...])
    a_col = -jnp.exp(alogT_ref[...])

    row = lax.broadcasted_iota(jnp.int32, (cl, cl), 0)
    col = lax.broadcasted_iota(jnp.int32, (cl, cl), 1)
    causal = col <= row
    tril = causal.astype(F32)
    triu = (row <= col).astype(F32)
    lane = lax.broadcasted_iota(jnp.int32, (cl, LANES), 1)
    left_head = lane < SSD_HEAD_DIM

    for c in range(step_len // cl):
        r0 = c * cl
        xs = xc[r0:r0 + cl, :SSD_D_INNER]
        bm = xc[r0:r0 + cl, SSD_D_INNER:SSD_D_INNER + SSD_BC].astype(BF16)
        cm = xc[r0:r0 + cl, SSD_D_INNER + SSD_BC:].astype(BF16)
        dt = dt_all[r0:r0 + cl]
        dtT = dtT_all[:, r0:r0 + cl]
        acum = jnp.dot(tril, dt * a_row, preferred_element_type=F32, precision=lax.Precision.HIGHEST)
        acumT = jnp.dot(dtT * a_col, triu, preferred_element_type=F32, precision=lax.Precision.HIGHEST)
        acum_x = _expand_heads(acum)
        last_x = acum_x[cl - 1:cl, :]
        xw = (xs * (_expand_heads(dt) * jnp.exp(last_x - acum_x))).astype(BF16)
        y_off_scale = jnp.exp(acum_x)
        state_decay = jnp.exp(last_x)

        y_parts = []
        for g in range(SSD_GROUPS):
            bg = bm[:, g * SSD_STATE:(g + 1) * SSD_STATE]
            cg = cm[:, g * SSD_STATE:(g + 1) * SSD_STATE]
            gsl = slice(g * SSD_HPG * SSD_HEAD_DIM, (g + 1) * SSD_HPG * SSD_HEAD_DIM)
            cb = lax.dot_general(cg, bg, NT_DIMS, preferred_element_type=F32)
            st = state_ref[:, gsl]
            y_off = jnp.dot(cg, st.astype(BF16), preferred_element_type=F32) * y_off_scale[:, gsl]
            for j in range(SSD_HPG // 2):
                mats = []
                for h in (g * SSD_HPG + 2 * j, g * SSD_HPG + 2 * j + 1):
                    seg = acum[:, h:h + 1] - acumT[h:h + 1, :]
                    decay = jnp.exp(jnp.where(causal, seg, -jnp.inf))
                    mats.append(cb * decay * dtT[h:h + 1, :])
                lhs = jnp.concatenate(mats, axis=1).astype(BF16)
                pair = slice((g * SSD_HPG + 2 * j) * SSD_HEAD_DIM, (g * SSD_HPG + 2 * j + 2) * SSD_HEAD_DIM)
                xp = xs[:, pair]
                rhs = jnp.concatenate([jnp.where(left_head, xp, 0.0), jnp.where(left_head, 0.0, xp)],
                                      axis=0).astype(BF16)
                y_diag = jnp.dot(lhs, rhs, preferred_element_type=F32)
                lsl = slice(2 * j * SSD_HEAD_DIM, (2 * j + 2) * SSD_HEAD_DIM)
                y_parts.append(y_diag + y_off[:, lsl])
            state_ref[:, gsl] = st * state_decay[:, gsl] + lax.dot_general(
                bg, xw[:, gsl], TN_DIMS, preferred_element_type=F32)
        y = jnp.concatenate(y_parts, axis=1) + xs * dskip_ref[...]
        y_ref[r0:r0 + cl, :] = y.astype(BF16)


def _ssd(xbc, dt, dtT, w, bsz, seq):
    ls = SSD_STEP
    consts = (w["conv_w"], w["conv_b"], w["dt_bias"], w["dt_bias_col"], w["a_log"], w["a_log_col"], w["d_skip_x"])
    return pl.pallas_call(
        _ssd_kernel,
        grid=(bsz, seq // ls),
        in_specs=[
            pl.BlockSpec((None, ls, SSD_CONV_DIM), lambda b, i: (b, i, 0)),
            pl.BlockSpec((None, ls, SSD_HEADS), lambda b, i: (b, i, 0)),
            pl.BlockSpec((None, SSD_HEADS, ls), lambda b, i: (b, 0, i)),
        ] + [_resident(x.shape) for x in consts],
        out_specs=pl.BlockSpec((None, ls, SSD_D_INNER), lambda b, i: (b, i, 0)),
        out_shape=jax.ShapeDtypeStruct((bsz, seq, SSD_D_INNER), BF16),
        scratch_shapes=[
            pltpu.VMEM((SSD_STATE, SSD_D_INNER), F32),
            pltpu.VMEM((ls + CONV_HALO, SSD_CONV_DIM), F32),
        ],
        compiler_params=_params(2),
        name="ssd",
    )(xbc, dt, dtT, *consts)


def _merge_kernel(h_ref, y_ref, zs_ref, o_ref, gate_ref, gssd_ref, wssd_ref, wmla_ref, wo_ref, out_ref):
    yg = y_ref[...].astype(F32) * zs_ref[...].astype(F32)
    y_ssd = jnp.dot(_rms(yg, gssd_ref[...]).astype(BF16), wssd_ref[...], preferred_element_type=F32)
    y_mla = jnp.dot(o_ref[...], wmla_ref[...], preferred_element_type=F32)
    gate = gate_ref[...].astype(F32)
    merged = gate[:, :D_MODEL] * y_ssd + gate[:, D_MODEL:] * y_mla
    out_ref[...] = h_ref[...] + jnp.dot(merged.astype(BF16), wo_ref[...], preferred_element_type=F32)


def _merge(h, y, zs, o, gate, w):
    t = h.shape[0]
    tm = TOKEN_TILE
    consts = (w["g_ssd"], w["w_ssd_out"], w["w_mla_out"], w["w_o"])

    def rows(width):
        return pl.BlockSpec((tm, width), lambda i: (i, 0))

    return pl.pallas_call(
        _merge_kernel,
        grid=(t // tm,),
        in_specs=[rows(D_MODEL), rows(SSD_D_INNER), rows(SSD_D_INNER), rows(MLA_HEADS * MLA_V), rows(2 * D_MODEL)]
        + [_resident(x.shape) for x in consts],
        out_specs=rows(D_MODEL),
        out_shape=jax.ShapeDtypeStruct((t, D_MODEL), F32),
        compiler_params=_params(1),
        name="merge",
    )(h, y, zs, o, gate, *consts)


def _split_cols(t, sizes):
    out, start = [], 0
    for n in sizes:
        out.append(t[..., start:start + n])
        start += n
    return out


def _layer_weights(l, p):
    half = MLA_ROPE // 2
    w_z, w_xbc, w_dt, w_cq, w_ckv, w_kr, w_gate = _split_cols(
        p["w_in"][l], (SSD_D_INNER, SSD_CONV_DIM, SSD_HEADS, MLA_Q_LORA, MLA_KV_LORA, MLA_ROPE, 2 * D_MODEL))
    w_krx = jnp.concatenate([w_kr, w_kr[:, half:], w_kr[:, :half]], axis=1)
    w_ukv = p["w_ukv"][l].reshape(MLA_KV_LORA, MLA_HEADS, MLA_NOPE + MLA_V)
    w_uk = w_ukv[:, :, :MLA_NOPE].reshape(MLA_KV_LORA, MLA_HEADS * MLA_NOPE)
    w_uv = w_ukv[:, :, MLA_NOPE:].reshape(MLA_KV_LORA, MLA_HEADS * MLA_V)
    g_q = p["q_norm"][l] * (MLA_QK ** -0.5)
    g_k = p["k_norm"][l]
    g1, g2 = g_k[MLA_NOPE:MLA_NOPE + half], g_k[MLA_NOPE + half:]
    rope_inv = 1.0 / (ROPE_THETA ** (jnp.arange(0, MLA_ROPE, 2, dtype=F32) / MLA_ROPE))
    return {
        "g_ffn1": p["ln_ffn1"][l][None, :],
        "ffn1_w13": p["ffn1_w13"][l].astype(BF16),
        "ffn1_w2": p["ffn1_w2"][l].astype(BF16),
        "g_ffn2": p["ln_ffn2"][l][None, :],
        "ffn2_w13": p["ffn2_w13"][l].astype(BF16),
        "ffn2_w2": p["ffn2_w2"][l].astype(BF16),
        "g_mix": p["ln_mix"][l][None, :],
        "w_z": w_z.astype(BF16),
        "w_xbc": w_xbc.astype(BF16),
        "w_dt": w_dt.astype(BF16),
        "w_dtT": w_dt.T.astype(BF16),
        "w_cq": w_cq.astype(BF16),
        "w_ckv": w_ckv.astype(BF16),
        "w_krx": w_krx.astype(BF16),
        "w_gate": w_gate.astype(BF16),
        "conv_w": p["conv_w"][l],
        "conv_b": p["conv_b"][l][None, :],
        "dt_bias": p["dt_bias"][l][None, :],
        "dt_bias_col": p["dt_bias"][l][:, None],
        "a_log": p["a_log"][l][None, :],
        "a_log_col": p["a_log"][l][:, None],
        "d_skip_x": jnp.repeat(p["d_skip"][l], SSD_HEAD_DIM)[None, :],
        "g_ssd": p["ssd_norm"][l][None, :],
        "w_ssd_out": p["w_ssd_out"][l].astype(BF16),
        "rope_inv": rope_inv[:, None],
        "g_q_lora": p["q_lora_norm"][l][None, :],
        "w_uqT": p["w_uq"][l].T.astype(BF16),
        "g_kv_lora": p["kv_lora_norm"][l][None, :],
        "w_uk": w_uk.astype(BF16),
        "w_uvT": w_uv.T.astype(BF16),
        "g_q_col": g_q[:, None],
        "g_k_nope": g_k[None, :MLA_NOPE],
        "g_k_rope_col": jnp.concatenate([g1, g2, -g2, g1])[:, None],
        "w_mla_out": p["w_mla_out"][l].astype(BF16),
        "w_o": p["w_o"][l].astype(BF16),
    }


def kernel(x, positions, ln_ffn1, ffn1_w13, ffn1_w2, ln_mix, w_in, conv_w, conv_b, dt_bias, a_log, d_skip,
           ssd_norm, w_ssd_out, q_lora_norm, w_uq, kv_lora_norm, w_ukv, q_norm, k_norm, w_mla_out, w_o,
           ln_ffn2, ffn2_w13, ffn2_w2):
    bsz, seq, d_model = x.shape
    assert d_model == D_MODEL and seq % ATTN_BLOCK == 0 and seq % SSD_STEP == 0 and seq % TOKEN_TILE == 0
    params = dict(ln_ffn1=ln_ffn1, ffn1_w13=ffn1_w13, ffn1_w2=ffn1_w2, ln_mix=ln_mix, w_in=w_in, conv_w=conv_w,
                  conv_b=conv_b, dt_bias=dt_bias, a_log=a_log, d_skip=d_skip, ssd_norm=ssd_norm,
                  w_ssd_out=w_ssd_out, q_lora_norm=q_lora_norm, w_uq=w_uq, kv_lora_norm=kv_lora_norm, w_ukv=w_ukv,
                  q_norm=q_norm, k_norm=k_norm, w_mla_out=w_mla_out, w_o=w_o, ln_ffn2=ln_ffn2, ffn2_w13=ffn2_w13,
                  ffn2_w2=ffn2_w2)
    tokens = bsz * seq
    pos_row = positions.reshape(bsz, 1, seq)
    h = x.reshape(tokens, D_MODEL)
    for l in range(ln_ffn1.shape[0]):
        w = _layer_weights(l, params)
        h = _ffn(h, w["g_ffn1"], w["ffn1_w13"], w["ffn1_w2"])
        zs, xbc, dt, dtT, cq, ckv, krx, gate = _in_proj(h, w["g_mix"], w, bsz, seq)
        y = _ssd(xbc, dt, dtT, w, bsz, seq)
        qT, k, vT = _mla_prep(cq, ckv, krx, pos_row, w, bsz, seq)
        o = _flash(qT, k, vT, bsz, seq)
        h = _merge(h, y.reshape(tokens, SSD_D_INNER), zs.reshape(tokens, SSD_D_INNER),
                   o.reshape(tokens, MLA_HEADS * MLA_V), gate.reshape(tokens, 2 * D_MODEL), w)
        h = _ffn(h, w["g_ffn2"], w["ffn2_w13"], w["ffn2_w2"])
    return h.reshape(bsz, seq, D_MODEL)
```

```python
import functools

import jax
import jax.numpy as jnp
import numpy as np
from jax import lax
from jax.experimental import pallas as pl
from jax.experimental.pallas import tpu as pltpu

F32 = jnp.float32
BF16 = jnp.bfloat16

D_MODEL = 1024
D_FF = 2816
SSD_D_INNER = 2048
SSD_HEAD_DIM = 64
SSD_HEADS = 32
SSD_GROUPS = 4
SSD_HPG = 8
SSD_STATE = 128
SSD_CONV = 4
SSD_CHUNK = 128
SSD_BC = SSD_GROUPS * SSD_STATE
SSD_CONV_DIM = SSD_D_INNER + 2 * SSD_BC
MLA_HEADS = 8
MLA_Q_LORA = 512
MLA_KV_LORA = 256
MLA_NOPE = 128
MLA_ROPE = 64
MLA_V = 128
MLA_QK = MLA_NOPE + MLA_ROPE
MLA_QK_PAD = 256
ROPE_THETA = 10000.0
EPS = 1e-6

LANES = 128
VMEM_LIMIT_BYTES = 56 * 1024 * 1024

TOKEN_TILE = 256
SSD_STEP = 256
ATTN_BLOCK = 512
CONV_HALO = 8

NT_DIMS = (((1,), (1,)), ((), ()))
TN_DIMS = (((0,), (0,)), ((), ()))


def _params(n_grid_axes):
    return pltpu.CompilerParams(
        dimension_semantics=("arbitrary",) * n_grid_axes,
        vmem_limit_bytes=VMEM_LIMIT_BYTES,
    )


def _resident(shape):
    zeros = (0,) * len(shape)
    return pl.BlockSpec(shape, lambda *_: zeros, pipeline_mode=pl.Buffered(1))


def _rms(x, gain):
    return x * lax.rsqrt(jnp.mean(x * x, axis=-1, keepdims=True) + EPS) * gain


def _silu(x):
    return x * jax.nn.sigmoid(x)


def _ffn_kernel(h_ref, g_ref, w13_ref, w2_ref, o_ref):
    x = h_ref[...]
    xb = _rms(x, g_ref[...]).astype(BF16)
    gate = jnp.dot(xb, w13_ref[:, :D_FF], preferred_element_type=F32)
    up = jnp.dot(xb, w13_ref[:, D_FF:], preferred_element_type=F32)
    hidden = (_silu(gate) * up).astype(BF16)
    o_ref[...] = x + 0.5 * jnp.dot(hidden, w2_ref[...], preferred_element_type=F32)


def _ffn(h, gain, w13, w2):
    t = h.shape[0]
    return pl.pallas_call(
        _ffn_kernel,
        grid=(t // TOKEN_TILE,),
        in_specs=[
            pl.BlockSpec((TOKEN_TILE, D_MODEL), lambda i: (i, 0)),
            _resident((1, D_MODEL)),
            _resident((D_MODEL, 2 * D_FF)),
            _resident((D_FF, D_MODEL)),
        ],
        out_specs=pl.BlockSpec((TOKEN_TILE, D_MODEL), lambda i: (i, 0)),
        out_shape=jax.ShapeDtypeStruct((t, D_MODEL), F32),
        compiler_params=_params(1),
        name="ffn",
    )(h, gain, w13, w2)


def _in_proj_kernel(h_ref, g_ref, wz_ref, wxbc_ref, wdt_ref, wdtT_ref, wcq_ref, wckv_ref, wkrx_ref, wg_ref,
                    zs_ref, xbc_ref, dt_ref, dtT_ref, cq_ref, ckv_ref, krx_ref, gate_ref):
    xb = _rms(h_ref[...], g_ref[...]).astype(BF16)

    def proj(w_ref):
        return jnp.dot(xb, w_ref[...], preferred_element_type=F32)

    zs_ref[...] = _silu(proj(wz_ref)).astype(BF16)
    xbc_ref[...] = proj(wxbc_ref).astype(BF16)
    dt_ref[...] = proj(wdt_ref)
    dtT_ref[...] = lax.dot_general(wdtT_ref[...], xb, NT_DIMS, preferred_element_type=F32)
    cq_ref[...] = proj(wcq_ref).astype(BF16)
    ckv_ref[...] = proj(wckv_ref).astype(BF16)
    krx_ref[...] = proj(wkrx_ref)
    gate_ref[...] = jax.nn.sigmoid(proj(wg_ref)).astype(BF16)


def _in_proj(h, gain, w, bsz, seq):
    tm = TOKEN_TILE
    nt = seq // tm
    widths = (SSD_D_INNER, SSD_CONV_DIM, SSD_HEADS, None, MLA_Q_LORA, MLA_KV_LORA, LANES, 2 * D_MODEL)
    dtypes = (BF16, BF16, F32, F32, BF16, BF16, F32, BF16)
    out_specs, out_shape = [], []
    for width, dtype in zip(widths, dtypes):
        if width is None:
            out_specs.append(pl.BlockSpec((None, SSD_HEADS, tm), lambda b, i: (b, 0, i)))
            out_shape.append(jax.ShapeDtypeStruct((bsz, SSD_HEADS, seq), dtype))
        else:
            out_specs.append(pl.BlockSpec((None, tm, width), lambda b, i: (b, i, 0)))
            out_shape.append(jax.ShapeDtypeStruct((bsz, seq, width), dtype))
    weights = (w["w_z"], w["w_xbc"], w["w_dt"], w["w_dtT"], w["w_cq"], w["w_ckv"], w["w_krx"], w["w_gate"])
    return pl.pallas_call(
        _in_proj_kernel,
        grid=(bsz, nt),
        in_specs=[pl.BlockSpec((None, tm, D_MODEL), lambda b, i: (b, i, 0)), _resident((1, D_MODEL))]
        + [_resident(x.shape) for x in weights],
        out_specs=out_specs,
        out_shape=out_shape,
        compiler_params=_params(2),
        name="in_proj",
    )(h.reshape(bsz, seq, D_MODEL), gain, *weights)


def _mla_prep_kernel(cq_ref, ckv_ref, krx_ref, pos_ref, inv_ref, gql_ref, wuqT_ref, gkvl_ref, wuk_ref, wuvT_ref,
                     gq_ref, gkn_ref, gkr_ref, qT_ref, k_ref, vT_ref):
    tm = cq_ref.shape[0]
    half = MLA_ROPE // 2
    ang = inv_ref[...] * pos_ref[...].astype(F32)
    cos, sin = jnp.cos(ang), jnp.sin(ang)

    qn = _rms(cq_ref[...].astype(F32), gql_ref[...]).astype(BF16)
    qT = lax.dot_general(wuqT_ref[...], qn, NT_DIMS, preferred_element_type=F32)
    gq = jnp.broadcast_to(gq_ref[...], (MLA_QK, tm))
    for h in range(MLA_HEADS):
        blk = qT[h * MLA_QK:(h + 1) * MLA_QK]
        inv_rms = lax.rsqrt(jnp.sum(blk * blk, axis=0, keepdims=True) * (1.0 / MLA_QK) + EPS)
        xn = blk * inv_rms * gq
        t1 = xn[MLA_NOPE:MLA_NOPE + half]
        t2 = xn[MLA_NOPE + half:]
        qT_ref[h, 0:MLA_NOPE, :] = xn[:MLA_NOPE].astype(BF16)
        qT_ref[h, MLA_NOPE:MLA_NOPE + half, :] = (t1 * cos - t2 * sin).astype(BF16)
        qT_ref[h, MLA_NOPE + half:MLA_QK, :] = (t2 * cos + t1 * sin).astype(BF16)
        qT_ref[h, MLA_QK:, :] = jnp.zeros((MLA_QK_PAD - MLA_QK, tm), BF16)

    kvn = _rms(ckv_ref[...].astype(F32), gkvl_ref[...]).astype(BF16)
    kn = jnp.dot(kvn, wuk_ref[...], preferred_element_type=F32)
    vT = lax.dot_general(wuvT_ref[...], kvn, NT_DIMS, preferred_element_type=F32)
    vT_ref[...] = vT.reshape(MLA_HEADS, MLA_V, tm).astype(BF16)

    krx = krx_ref[...]
    rope_ss = 0.5 * jnp.sum(krx * krx, axis=-1, keepdims=True)
    table = (jnp.concatenate([cos, cos, sin, sin], axis=0) * gkr_ref[...]).T
    prod = krx * table
    k_rope = prod + pltpu.roll(prod, MLA_ROPE, 1)
    gkn = gkn_ref[...]
    for h in range(MLA_HEADS):
        knh = kn[:, h * MLA_NOPE:(h + 1) * MLA_NOPE]
        ss = jnp.sum(knh * knh, axis=-1, keepdims=True) + rope_ss
        inv_rms = lax.rsqrt(ss * (1.0 / MLA_QK) + EPS)
        k_ref[h, :, 0:MLA_NOPE] = (knh * inv_rms * gkn).astype(BF16)
        k_ref[h, :, MLA_NOPE:] = (k_rope * inv_rms).astype(BF16)


def _mla_prep(cq, ckv, krx, pos_row, w, bsz, seq):
    tm = ATTN_BLOCK
    nt = seq // tm
    consts = (w["rope_inv"], w["g_q_lora"], w["w_uqT"], w["g_kv_lora"], w["w_uk"], w["w_uvT"],
              w["g_q_col"], w["g_k_nope"], w["g_k_rope_col"])
    in_specs = [
        pl.BlockSpec((None, tm, MLA_Q_LORA), lambda b, i: (b, i, 0)),
        pl.BlockSpec((None, tm, MLA_KV_LORA), lambda b, i: (b, i, 0)),
        pl.BlockSpec((None, tm, LANES), lambda b, i: (b, i, 0)),
        pl.BlockSpec((None, 1, tm), lambda b, i: (b, 0, i)),
    ] + [_resident(x.shape) for x in consts]
    out_specs = [
        pl.BlockSpec((None, MLA_HEADS, MLA_QK_PAD, tm), lambda b, i: (b, 0, 0, i)),
        pl.BlockSpec((None, MLA_HEADS, tm, MLA_QK_PAD), lambda b, i: (b, 0, i, 0)),
        pl.BlockSpec((None, MLA_HEADS, None, MLA_V, tm), lambda b, i: (b, 0, i, 0, 0)),
    ]
    out_shape = [
        jax.ShapeDtypeStruct((bsz, MLA_HEADS, MLA_QK_PAD, seq), BF16),
        jax.ShapeDtypeStruct((bsz, MLA_HEADS, seq, MLA_QK_PAD), BF16),
        jax.ShapeDtypeStruct((bsz, MLA_HEADS, nt, MLA_V, tm), BF16),
    ]
    return pl.pallas_call(
        _mla_prep_kernel,
        grid=(bsz, nt),
        in_specs=in_specs,
        out_specs=out_specs,
        out_shape=out_shape,
        compiler_params=_params(2),
        name="mla_prep",
    )(cq, ckv, krx, pos_row, *consts)


def _flash_kernel(qT_ref, k_ref, vT_ref, o_ref, s_ref, m_ref, l_ref, acc_ref):
    blk = ATTN_BLOCK
    i = pl.program_id(2)
    m_ref[...] = jnp.full(m_ref.shape, -jnp.inf, F32)
    l_ref[...] = jnp.zeros(l_ref.shape, F32)
    acc_ref[...] = jnp.zeros(acc_ref.shape, F32)
    qT = qT_ref[...]

    def scores(j, slot):
        k_blk = k_ref[pl.ds(pl.multiple_of(j * blk, blk), blk), :]
        s_ref[slot] = jnp.dot(k_blk, qT, preferred_element_type=F32)

    def consume(j, slot, on_diagonal):
        sT = s_ref[slot]
        if on_diagonal:
            kpos = lax.broadcasted_iota(jnp.int32, (blk, blk), 0)
            qpos = lax.broadcasted_iota(jnp.int32, (blk, blk), 1)
            sT = jnp.where(kpos <= qpos, sT, -jnp.inf)
        m_prev = m_ref[...]
        m_new = jnp.maximum(m_prev, jnp.max(sT, axis=0, keepdims=True))
        alpha = jnp.exp2(m_prev - m_new)
        p = jnp.exp2(sT - m_new)
        l_ref[...] = alpha * l_ref[...] + jnp.sum(p, axis=0, keepdims=True)
        pv = jnp.dot(vT_ref[j], p.astype(BF16), preferred_element_type=F32)
        acc_ref[...] = alpha * acc_ref[...] + pv
        m_ref[...] = m_new

    def two_blocks(t, carry):
        j = 2 * t
        scores(j + 1, 1)
        consume(j, 0, False)
        scores(j + 2, 0)
        consume(j + 1, 1, False)
        return carry

    scores(0, 0)
    lax.fori_loop(0, lax.shift_right_logical(i, 1), two_blocks, 0)
    odd = (i & 1) == 1

    @pl.when(odd)
    def _():
        scores(i, 1)
        consume(i - 1, 0, False)
        consume(i, 1, True)

    @pl.when(jnp.logical_not(odd))
    def _():
        consume(i, 0, True)

    o_ref[...] = (acc_ref[...] / l_ref[...]).T.astype(BF16)


def _flash(qT, k, vT, bsz, seq):
    blk = ATTN_BLOCK
    nb = seq // blk
    return pl.pallas_call(
        _flash_kernel,
        grid=(bsz, MLA_HEADS, nb),
        in_specs=[
            pl.BlockSpec((None, None, MLA_QK_PAD, blk), lambda b, h, i: (b, h, 0, i)),
            pl.BlockSpec((None, None, seq, MLA_QK_PAD), lambda b, h, i: (b, h, 0, 0)),
            pl.BlockSpec((None, None, nb, MLA_V, blk), lambda b, h, i: (b, h, 0, 0, 0)),
        ],
        out_specs=pl.BlockSpec((None, blk, MLA_V), lambda b, h, i: (b, i, h)),
        out_shape=jax.ShapeDtypeStruct((bsz, seq, MLA_HEADS * MLA_V), BF16),
        scratch_shapes=[
            pltpu.VMEM((2, blk, blk), F32),
            pltpu.VMEM((1, blk), F32),
            pltpu.VMEM((1, blk), F32),
            pltpu.VMEM((MLA_V, blk), F32),
        ],
        compiler_params=_params(3),
        name="flash",
    )(qT, k, vT)


def _softplus(x):
    return jnp.maximum(x, 0.0) + jnp.log1p(jnp.exp(-jnp.abs(x)))


def _expand_heads(v):
    rows = v.shape[0]
    lane = lax.broadcasted_iota(jnp.int32, (rows, LANES), 1)
    parts = []
    for j in range(SSD_HEADS // 2):
        parts.append(jnp.where(lane < SSD_HEAD_DIM, v[:, 2 * j:2 * j + 1], v[:, 2 * j + 1:2 * j + 2]))
    return jnp.concatenate(parts, axis=1)


def _ssd_kernel(xbc_ref, dt_ref, dtT_ref, convw_ref, convb_ref, dtb_ref, dtbT_ref, alog_ref, alogT_ref, dskip_ref,
                y_ref, state_ref, ext_ref):
    step_len = xbc_ref.shape[0]
    cl = SSD_CHUNK

    @pl.when(pl.program_id(1) == 0)
    def _():
        state_ref[...] = jnp.zeros(state_ref.shape, F32)
        ext_ref[0:CONV_HALO, :] = jnp.zeros((CONV_HALO, SSD_CONV_DIM), F32)

    ext_ref[CONV_HALO:, :] = xbc_ref[...].astype(F32)
    conv = convb_ref[...]
    for k in range(SSD_CONV):
        start = CONV_HALO - (SSD_CONV - 1) + k
        conv = conv + convw_ref[k:k + 1, :] * ext_ref[start:start + step_len, :]
    ext_ref[0:CONV_HALO, :] = ext_ref[step_len:step_len + CONV_HALO, :]
    xc = _silu(conv)

    dt_all = _softplus(dt_ref[...] + dtb_ref[...])
    dtT_all = _softplus(dtT_ref[...] + dtbT_ref[...])
    a_row = -jnp.exp(alog_ref[...])
    a_col = -jnp.exp(alogT_ref[...])

    row = lax.broadcasted_iota(jnp.int32, (cl, cl), 0)
    col = lax.broadcasted_iota(jnp.int32, (cl, cl), 1)
    causal = col <= row
    tril = causal.astype(F32)
    triu = (row <= col).astype(F32)
    lane = lax.broadcasted_iota(jnp.int32, (cl, LANES), 1)
    left_head = lane < SSD_HEAD_DIM

    for c in range(step_len // cl):
        r0 = c * cl
        xs = xc[r0:r0 + cl, :SSD_D_INNER]
        bm = xc[r0:r0 + cl, SSD_D_INNER:SSD_D_INNER + SSD_BC].astype(BF16)
        cm = xc[r0:r0 + cl, SSD_D_INNER + SSD_BC:].astype(BF16)
        dt = dt_all[r0:r0 + cl]
        dtT = dtT_all[:, r0:r0 + cl]
        acum = jnp.dot(tril, dt * a_row, preferred_element_type=F32, precision=lax.Precision.HIGHEST)
        acumT = jnp.dot(dtT * a_col, triu, preferred_element_type=F32, precision=lax.Precision.HIGHEST)
        acum_x = _expand_heads(acum)
        last_x = acum_x[cl - 1:cl, :]
        xw = (xs * (_expand_heads(dt) * jnp.exp(last_x - acum_x))).astype(BF16)
        y_off_scale = jnp.exp(acum_x)
        state_decay = jnp.exp(last_x)

        y_parts = []
        for g in range(SSD_GROUPS):
            bg = bm[:, g * SSD_STATE:(g + 1) * SSD_STATE]
            cg = cm[:, g * SSD_STATE:(g + 1) * SSD_STATE]
            gsl = slice(g * SSD_HPG * SSD_HEAD_DIM, (g + 1) * SSD_HPG * SSD_HEAD_DIM)
            cb = lax.dot_general(cg, bg, NT_DIMS, preferred_element_type=F32)
            st = state_ref[:, gsl]
            y_off = jnp.dot(cg, st.astype(BF16), preferred_element_type=F32) * y_off_scale[:, gsl]
            for j in range(SSD_HPG // 2):
                mats = []
                for h in (g * SSD_HPG + 2 * j, g * SSD_HPG + 2 * j + 1):
                    seg = acum[:, h:h + 1] - acumT[h:h + 1, :]
                    decay = jnp.exp(jnp.where(causal, seg, -jnp.inf))
                    mats.append(cb * decay * dtT[h:h + 1, :])
                lhs = jnp.concatenate(mats, axis=1).astype(BF16)
                pair = slice((g * SSD_HPG + 2 * j) * SSD_HEAD_DIM, (g * SSD_HPG + 2 * j + 2) * SSD_HEAD_DIM)
                xp = xs[:, pair]
                rhs = jnp.concatenate([jnp.where(left_head, xp, 0.0), jnp.where(left_head, 0.0, xp)],
                                      axis=0).astype(BF16)
                y_diag = jnp.dot(lhs, rhs, preferred_element_type=F32)
                lsl = slice(2 * j * SSD_HEAD_DIM, (2 * j + 2) * SSD_HEAD_DIM)
                y_parts.append(y_diag + y_off[:, lsl])
            state_ref[:, gsl] = st * state_decay[:, gsl] + lax.dot_general(
                bg, xw[:, gsl], TN_DIMS, preferred_element_type=F32)
        y = jnp.concatenate(y_parts, axis=1) + xs * dskip_ref[...]
        y_ref[r0:r0 + cl, :] = y.astype(BF16)


def _ssd(xbc, dt, dtT, w, bsz, seq):
    ls = SSD_STEP
    consts = (w["conv_w"], w["conv_b"], w["dt_bias"], w["dt_bias_col"], w["a_log"], w["a_log_col"], w["d_skip_x"])
    return pl.pallas_call(
        _ssd_kernel,
        grid=(bsz, seq // ls),
        in_specs=[
            pl.BlockSpec((None, ls, SSD_CONV_DIM), lambda b, i: (b, i, 0)),
            pl.BlockSpec((None, ls, SSD_HEADS), lambda b, i: (b, i, 0)),
            pl.BlockSpec((None, SSD_HEADS, ls), lambda b, i: (b, 0, i)),
        ] + [_resident(x.shape) for x in consts],
        out_specs=pl.BlockSpec((None, ls, SSD_D_INNER), lambda b, i: (b, i, 0)),
        out_shape=jax.ShapeDtypeStruct((bsz, seq, SSD_D_INNER), BF16),
        scratch_shapes=[
            pltpu.VMEM((SSD_STATE, SSD_D_INNER), F32),
            pltpu.VMEM((ls + CONV_HALO, SSD_CONV_DIM), F32),
        ],
        compiler_params=_params(2),
        name="ssd",
    )(xbc, dt, dtT, *consts)


def _merge_kernel(h_ref, y_ref, zs_ref, o_ref, gate_ref, gssd_ref, wssd_ref, wmla_ref, wo_ref, out_ref):
    yg = y_ref[...].astype(F32) * zs_ref[...].astype(F32)
    y_ssd = jnp.dot(_rms(yg, gssd_ref[...]).astype(BF16), wssd_ref[...], preferred_element_type=F32)
    y_mla = jnp.dot(o_ref[...], wmla_ref[...], preferred_element_type=F32)
    gate = gate_ref[...].astype(F32)
    merged = gate[:, :D_MODEL] * y_ssd + gate[:, D_MODEL:] * y_mla
    out_ref[...] = h_ref[...] + jnp.dot(merged.astype(BF16), wo_ref[...], preferred_element_type=F32)


def _merge(h, y, zs, o, gate, w):
    t = h.shape[0]
    tm = TOKEN_TILE
    consts = (w["g_ssd"], w["w_ssd_out"], w["w_mla_out"], w["w_o"])

    def rows(width):
        return pl.BlockSpec((tm, width), lambda i: (i, 0))

    return pl.pallas_call(
        _merge_kernel,
        grid=(t // tm,),
        in_specs=[rows(D_MODEL), rows(SSD_D_INNER), rows(SSD_D_INNER), rows(MLA_HEADS * MLA_V), rows(2 * D_MODEL)]
        + [_resident(x.shape) for x in consts],
        out_specs=rows(D_MODEL),
        out_shape=jax.ShapeDtypeStruct((t, D_MODEL), F32),
        compiler_params=_params(1),
        name="merge",
    )(h, y, zs, o, gate, *consts)


def _split_cols(t, sizes):
    out, start = [], 0
    for n in sizes:
        out.append(t[..., start:start + n])
        start += n
    return out


def _layer_weights(l, p):
    half = MLA_ROPE // 2
    w_z, w_xbc, w_dt, w_cq, w_ckv, w_kr, w_gate = _split_cols(
        p["w_in"][l], (SSD_D_INNER, SSD_CONV_DIM, SSD_HEADS, MLA_Q_LORA, MLA_KV_LORA, MLA_ROPE, 2 * D_MODEL))
    w_krx = jnp.concatenate([w_kr, w_kr[:, half:], w_kr[:, :half]], axis=1)
    w_ukv = p["w_ukv"][l].reshape(MLA_KV_LORA, MLA_HEADS, MLA_NOPE + MLA_V)
    w_uk = w_ukv[:, :, :MLA_NOPE].reshape(MLA_KV_LORA, MLA_HEADS * MLA_NOPE)
    w_uv = w_ukv[:, :, MLA_NOPE:].reshape(MLA_KV_LORA, MLA_HEADS * MLA_V)
    g_q = p["q_norm"][l] * float(MLA_QK ** -0.5 * np.log2(np.e))
    g_k = p["k_norm"][l]
    g1, g2 = g_k[MLA_NOPE:MLA_NOPE + half], g_k[MLA_NOPE + half:]
    rope_inv = 1.0 / (ROPE_THETA ** (jnp.arange(0, MLA_ROPE, 2, dtype=F32) / MLA_ROPE))
    return {
        "g_ffn1": p["ln_ffn1"][l][None, :],
        "ffn1_w13": p["ffn1_w13"][l].astype(BF16),
        "ffn1_w2": p["ffn1_w2"][l].astype(BF16),
        "g_ffn2": p["ln_ffn2"][l][None, :],
        "ffn2_w13": p["ffn2_w13"][l].astype(BF16),
        "ffn2_w2": p["ffn2_w2"][l].astype(BF16),
        "g_mix": p["ln_mix"][l][None, :],
        "w_z": w_z.astype(BF16),
        "w_xbc": w_xbc.astype(BF16),
        "w_dt": w_dt.astype(BF16),
        "w_dtT": w_dt.T.astype(BF16),
        "w_cq": w_cq.astype(BF16),
        "w_ckv": w_ckv.astype(BF16),
        "w_krx": w_krx.astype(BF16),
        "w_gate": w_gate.astype(BF16),
        "conv_w": p["conv_w"][l],
        "conv_b": p["conv_b"][l][None, :],
        "dt_bias": p["dt_bias"][l][None, :],
        "dt_bias_col": p["dt_bias"][l][:, None],
        "a_log": p["a_log"][l][None, :],
        "a_log_col": p["a_log"][l][:, None],
        "d_skip_x": jnp.repeat(p["d_skip"][l], SSD_HEAD_DIM)[None, :],
        "g_ssd": p["ssd_norm"][l][None, :],
        "w_ssd_out": p["w_ssd_out"][l].astype(BF16),
        "rope_inv": rope_inv[:, None],
        "g_q_lora": p["q_lora_norm"][l][None, :],
        "w_uqT": p["w_uq"][l].T.astype(BF16),
        "g_kv_lora": p["kv_lora_norm"][l][None, :],
        "w_uk": w_uk.astype(BF16),
        "w_uvT": w_uv.T.astype(BF16),
        "g_q_col": g_q[:, None],
        "g_k_nope": g_k[None, :MLA_NOPE],
        "g_k_rope_col": jnp.concatenate([g1, g2, -g2, g1])[:, None],
        "w_mla_out": p["w_mla_out"][l].astype(BF16),
        "w_o": p["w_o"][l].astype(BF16),
    }


def kernel(x, positions, ln_ffn1, ffn1_w13, ffn1_w2, ln_mix, w_in, conv_w, conv_b, dt_bias, a_log, d_skip,
           ssd_norm, w_ssd_out, q_lora_norm, w_uq, kv_lora_norm, w_ukv, q_norm, k_norm, w_mla_out, w_o,
           ln_ffn2, ffn2_w13, ffn2_w2):
    bsz, seq, d_model = x.shape
    assert d_model == D_MODEL and seq % ATTN_BLOCK == 0 and seq % SSD_STEP == 0 and seq % TOKEN_TILE == 0
    params = dict(ln_ffn1=ln_ffn1, ffn1_w13=ffn1_w13, ffn1_w2=ffn1_w2, ln_mix=ln_mix, w_in=w_in, conv_w=conv_w,
                  conv_b=conv_b, dt_bias=dt_bias, a_log=a_log, d_skip=d_skip, ssd_norm=ssd_norm,
                  w_ssd_out=w_ssd_out, q_lora_norm=q_lora_norm, w_uq=w_uq, kv_lora_norm=kv_lora_norm, w_ukv=w_ukv,
                  q_norm=q_norm, k_norm=k_norm, w_mla_out=w_mla_out, w_o=w_o, ln_ffn2=ln_ffn2, ffn2_w13=ffn2_w13,
                  ffn2_w2=ffn2_w2)
    tokens = bsz * seq
    pos_row = positions.reshape(bsz, 1, seq)
    h = x.reshape(tokens, D_MODEL)
    for l in range(ln_ffn1.shape[0]):
        w = _layer_weights(l, params)
        h = _ffn(h, w["g_ffn1"], w["ffn1_w13"], w["ffn1_w2"])
        zs, xbc, dt, dtT, cq, ckv, krx, gate = _in_proj(h, w["g_mix"], w, bsz, seq)
        y = _ssd(xbc, dt, dtT, w, bsz, seq)
        qT, k, vT = _mla_prep(cq, ckv, krx, pos_row, w, bsz, seq)
        o = _flash(qT, k, vT, bsz, seq)
        h = _merge(h, y.reshape(tokens, SSD_D_INNER), zs.reshape(tokens, SSD_D_INNER),
                   o.reshape(tokens, MLA_HEADS * MLA_V), gate.reshape(tokens, 2 * D_MODEL), w)
        h = _ffn(h, w["g_ffn2"], w["ffn2_w13"], w["ffn2_w2"])
    return h.reshape(bsz, seq, D_MODEL)
```

```python
import jax
import jax.numpy as jnp
import numpy as np
from jax import lax
from jax.experimental import pallas as pl
from jax.experimental.pallas import tpu as pltpu

F32 = jnp.float32
BF16 = jnp.bfloat16

D_MODEL = 1024
D_FF = 2816
SSD_D_INNER = 2048
SSD_HEAD_DIM = 64
SSD_HEADS = 32
SSD_GROUPS = 4
SSD_HPG = 8
SSD_STATE = 128
SSD_CONV = 4
SSD_CHUNK = 128
SSD_BC = SSD_GROUPS * SSD_STATE
SSD_CONV_DIM = SSD_D_INNER + 2 * SSD_BC
MLA_HEADS = 8
MLA_Q_LORA = 512
MLA_KV_LORA = 256
MLA_NOPE = 128
MLA_ROPE = 64
MLA_V = 128
MLA_QK = MLA_NOPE + MLA_ROPE
MLA_QK_PAD = 256
ROPE_THETA = 10000.0
EPS = 1e-6

LANES = 128
VMEM_LIMIT_BYTES = 56 * 1024 * 1024

TOKEN_TILE = 256
SSD_STEP = 256
ATTN_BLOCK = 512
FLASH_Q_BLOCK = 1024
CONV_HALO = 8

NT_DIMS = (((1,), (1,)), ((), ()))
TN_DIMS = (((0,), (0,)), ((), ()))


def _params(n_grid_axes):
    return pltpu.CompilerParams(
        dimension_semantics=("arbitrary",) * n_grid_axes,
        vmem_limit_bytes=VMEM_LIMIT_BYTES,
    )


def _resident(shape):
    zeros = (0,) * len(shape)
    return pl.BlockSpec(shape, lambda *_: zeros, pipeline_mode=pl.Buffered(1))


def _rms(x, gain):
    return x * lax.rsqrt(jnp.mean(x * x, axis=-1, keepdims=True) + EPS) * gain


def _silu(x):
    return x * jax.nn.sigmoid(x)


def _ffn_kernel(h_ref, g_ref, w13_ref, w2_ref, o_ref):
    x = h_ref[...]
    xb = _rms(x, g_ref[...]).astype(BF16)
    gate = jnp.dot(xb, w13_ref[:, :D_FF], preferred_element_type=F32)
    up = jnp.dot(xb, w13_ref[:, D_FF:], preferred_element_type=F32)
    hidden = (_silu(gate) * up).astype(BF16)
    o_ref[...] = x + 0.5 * jnp.dot(hidden, w2_ref[...], preferred_element_type=F32)


def _ffn(h, gain, w13, w2):
    t = h.shape[0]
    return pl.pallas_call(
        _ffn_kernel,
        grid=(t // TOKEN_TILE,),
        in_specs=[
            pl.BlockSpec((TOKEN_TILE, D_MODEL), lambda i: (i, 0)),
            _resident((1, D_MODEL)),
            _resident((D_MODEL, 2 * D_FF)),
            _resident((D_FF, D_MODEL)),
        ],
        out_specs=pl.BlockSpec((TOKEN_TILE, D_MODEL), lambda i: (i, 0)),
        out_shape=jax.ShapeDtypeStruct((t, D_MODEL), F32),
        compiler_params=_params(1),
        name="ffn",
    )(h, gain, w13, w2)


def _in_proj_kernel(h_ref, g_ref, wz_ref, wxbc_ref, wdt_ref, wdtT_ref, wcq_ref, wckv_ref, wkrx_ref, wg_ref,
                    convw_ref, convb_ref,
                    zs_ref, xc_ref, dt_ref, dtT_ref, cq_ref, ckv_ref, krx_ref, gate_ref, ext_ref, act_ref):
    tm = h_ref.shape[0]
    xb = _rms(h_ref[...], g_ref[...]).astype(BF16)

    def proj(w_ref):
        return jnp.dot(xb, w_ref[...], preferred_element_type=F32)

    zs_ref[...] = _silu(proj(wz_ref)).astype(BF16)
    gate_ref[...] = jax.nn.sigmoid(proj(wg_ref)).astype(BF16)

    n_slabs = SSD_CONV_DIM // LANES

    @pl.when(pl.program_id(1) == 0)
    def _():
        ext_ref[:, 0:CONV_HALO, :] = jnp.zeros((n_slabs, CONV_HALO, LANES), F32)

    xbc = proj(wxbc_ref)
    for sl in range(n_slabs):
        cs = slice(sl * LANES, (sl + 1) * LANES)
        ext_ref[sl, CONV_HALO:, :] = xbc[:, cs]
        for parity in range(2):
            conv = convb_ref[:, cs]
            for k in range(SSD_CONV):
                start = CONV_HALO - (SSD_CONV - 1) + k + parity
                conv = conv + convw_ref[k:k + 1, cs] * ext_ref[sl, pl.ds(start, tm // 2, stride=2), :]
            act_ref[sl, pl.ds(parity, tm // 2, stride=2), :] = _silu(conv)
        ext_ref[sl, 0:CONV_HALO, :] = ext_ref[sl, tm:tm + CONV_HALO, :]
        xc_ref[:, cs] = act_ref[sl].astype(BF16)

    dt_ref[...] = proj(wdt_ref)
    dtT_ref[...] = lax.dot_general(wdtT_ref[...], xb, NT_DIMS, preferred_element_type=F32)
    cq_ref[...] = proj(wcq_ref).astype(BF16)
    ckv_ref[...] = proj(wckv_ref).astype(BF16)
    krx_ref[...] = proj(wkrx_ref)


def _in_proj(h, gain, w, bsz, seq):
    tm = TOKEN_TILE
    nt = seq // tm
    widths = (SSD_D_INNER, SSD_CONV_DIM, LANES, None, MLA_Q_LORA, MLA_KV_LORA, LANES, 2 * D_MODEL)
    dtypes = (BF16, BF16, F32, F32, BF16, BF16, F32, BF16)
    out_specs, out_shape = [], []
    for width, dtype in zip(widths, dtypes):
        if width is None:
            out_specs.append(pl.BlockSpec((None, SSD_HEADS, tm), lambda b, i: (b, 0, i)))
            out_shape.append(jax.ShapeDtypeStruct((bsz, SSD_HEADS, seq), dtype))
        else:
            out_specs.append(pl.BlockSpec((None, tm, width), lambda b, i: (b, i, 0)))
            out_shape.append(jax.ShapeDtypeStruct((bsz, seq, width), dtype))
    weights = (w["w_z"], w["w_xbc"], w["w_dt4"], w["w_dtT"], w["w_cq"], w["w_ckv"], w["w_krx"], w["w_gate"],
               w["conv_w"], w["conv_b"])
    return pl.pallas_call(
        _in_proj_kernel,
        grid=(bsz, nt),
        in_specs=[pl.BlockSpec((None, tm, D_MODEL), lambda b, i: (b, i, 0)), _resident((1, D_MODEL))]
        + [_resident(x.shape) for x in weights],
        out_specs=out_specs,
        out_shape=out_shape,
        scratch_shapes=[pltpu.VMEM((SSD_CONV_DIM // LANES, tm + CONV_HALO, LANES), F32),
                        pltpu.VMEM((SSD_CONV_DIM // LANES, tm, LANES), F32)],
        compiler_params=_params(2),
        name="in_proj",
    )(h.reshape(bsz, seq, D_MODEL), gain, *weights)


def _mla_prep_kernel(cq_ref, ckv_ref, krx_ref, pos_ref, inv_ref, gql_ref, wuqT_ref, gkvl_ref, wuk_ref, wuvT_ref,
                     gq_ref, gkn_ref, gkr_ref, qT_ref, k_ref, vT_ref):
    tm = cq_ref.shape[0]
    half = MLA_ROPE // 2
    ang = inv_ref[...] * pos_ref[...].astype(F32)
    cos, sin = jnp.cos(ang), jnp.sin(ang)

    qn = _rms(cq_ref[...].astype(F32), gql_ref[...]).astype(BF16)
    qT = lax.dot_general(wuqT_ref[...], qn, NT_DIMS, preferred_element_type=F32)
    gq = jnp.broadcast_to(gq_ref[...], (MLA_QK, tm))
    for h in range(MLA_HEADS):
        blk = qT[h * MLA_QK:(h + 1) * MLA_QK]
        inv_rms = lax.rsqrt(jnp.sum(blk * blk, axis=0, keepdims=True) * (1.0 / MLA_QK) + EPS)
        xn = blk * inv_rms * gq
        t1 = xn[MLA_NOPE:MLA_NOPE + half]
        t2 = xn[MLA_NOPE + half:]
        qT_ref[h, 0:MLA_NOPE, :] = xn[:MLA_NOPE].astype(BF16)
        qT_ref[h, MLA_NOPE:MLA_NOPE + half, :] = (t1 * cos - t2 * sin).astype(BF16)
        qT_ref[h, MLA_NOPE + half:MLA_QK, :] = (t2 * cos + t1 * sin).astype(BF16)
        qT_ref[h, MLA_QK:, :] = jnp.zeros((MLA_QK_PAD - MLA_QK, tm), BF16)

    kvn = _rms(ckv_ref[...].astype(F32), gkvl_ref[...]).astype(BF16)
    kn = jnp.dot(kvn, wuk_ref[...], preferred_element_type=F32)
    vT = lax.dot_general(wuvT_ref[...], kvn, NT_DIMS, preferred_element_type=F32)
    vT_ref[...] = vT.reshape(MLA_HEADS, MLA_V, tm).astype(BF16)

    krx = krx_ref[...]
    rope_ss = 0.5 * jnp.sum(krx * krx, axis=-1, keepdims=True)
    table = (jnp.concatenate([cos, cos, sin, sin], axis=0) * gkr_ref[...]).T
    prod = krx * table
    k_rope = prod + pltpu.roll(prod, MLA_ROPE, 1)
    gkn = gkn_ref[...]
    for h in range(MLA_HEADS):
        knh = kn[:, h * MLA_NOPE:(h + 1) * MLA_NOPE]
        ss = jnp.sum(knh * knh, axis=-1, keepdims=True) + rope_ss
        inv_rms = lax.rsqrt(ss * (1.0 / MLA_QK) + EPS)
        k_ref[h, :, 0:MLA_NOPE] = (knh * inv_rms * gkn).astype(BF16)
        k_ref[h, :, MLA_NOPE:] = (k_rope * inv_rms).astype(BF16)


def _mla_prep(cq, ckv, krx, pos_row, w, bsz, seq):
    tm = ATTN_BLOCK
    nt = seq // tm
    consts = (w["rope_inv"], w["g_q_lora"], w["w_uqT"], w["g_kv_lora"], w["w_uk"], w["w_uvT"],
              w["g_q_col"], w["g_k_nope"], w["g_k_rope_col"])
    in_specs = [
        pl.BlockSpec((None, tm, MLA_Q_LORA), lambda b, i: (b, i, 0)),
        pl.BlockSpec((None, tm, MLA_KV_LORA), lambda b, i: (b, i, 0)),
        pl.BlockSpec((None, tm, LANES), lambda b, i: (b, i, 0)),
        pl.BlockSpec((None, 1, tm), lambda b, i: (b, 0, i)),
    ] + [_resident(x.shape) for x in consts]
    out_specs = [
        pl.BlockSpec((None, MLA_HEADS, MLA_QK_PAD, tm), lambda b, i: (b, 0, 0, i)),
        pl.BlockSpec((None, MLA_HEADS, tm, MLA_QK_PAD), lambda b, i: (b, 0, i, 0)),
        pl.BlockSpec((None, MLA_HEADS, None, MLA_V, tm), lambda b, i: (b, 0, i, 0, 0)),
    ]
    out_shape = [
        jax.ShapeDtypeStruct((bsz, MLA_HEADS, MLA_QK_PAD, seq), BF16),
        jax.ShapeDtypeStruct((bsz, MLA_HEADS, seq, MLA_QK_PAD), BF16),
        jax.ShapeDtypeStruct((bsz, MLA_HEADS, nt, MLA_V, tm), BF16),
    ]
    return pl.pallas_call(
        _mla_prep_kernel,
        grid=(bsz, nt),
        in_specs=in_specs,
        out_specs=out_specs,
        out_shape=out_shape,
        compiler_params=_params(2),
        name="mla_prep",
    )(cq, ckv, krx, pos_row, *consts)


def _flash_kernel(qT_ref, k_ref, vT_ref, o_ref, s_ref, cmax_ref, m_ref, l_ref, acc_ref):
    bq, bk = FLASH_Q_BLOCK, ATTN_BLOCK
    kpq = bq // bk
    i = pl.program_id(2)
    m_ref[...] = jnp.full(m_ref.shape, -jnp.inf, F32)
    l_ref[...] = jnp.zeros(l_ref.shape, F32)
    acc_ref[...] = jnp.zeros(acc_ref.shape, F32)
    qT = qT_ref[...]

    def scores(j, slot):
        k_blk = k_ref[pl.ds(pl.multiple_of(j * bk, bk), bk), :]
        sT = jnp.dot(k_blk, qT, preferred_element_type=F32)
        s_ref[slot] = sT
        cmax_ref[slot] = jnp.max(sT, axis=0, keepdims=True)

    def consume(j, slot, diag_offset):
        sT = s_ref[slot]
        if diag_offset is None:
            cmax = cmax_ref[slot]
        else:
            kpos = lax.broadcasted_iota(jnp.int32, (bk, bq), 0) + diag_offset * bk
            qpos = lax.broadcasted_iota(jnp.int32, (bk, bq), 1)
            sT = jnp.where(kpos <= qpos, sT, -jnp.inf)
            cmax = jnp.max(sT, axis=0, keepdims=True)
        m_prev = m_ref[...]
        m_new = jnp.maximum(m_prev, cmax)
        alpha = jnp.exp2(m_prev - m_new)
        p = jnp.exp2(sT - m_new)
        l_ref[...] = alpha * l_ref[...] + jnp.sum(p, axis=0, keepdims=True)
        pv = jnp.dot(vT_ref[j], p.astype(BF16), preferred_element_type=F32)
        acc_ref[...] = alpha * acc_ref[...] + pv
        m_ref[...] = m_new

    def two_blocks(t, carry):
        j = 2 * t
        scores(j + 1, 1)
        consume(j, 0, None)
        scores(j + 2, 0)
        consume(j + 1, 1, None)
        return carry

    scores(0, 0)
    lax.fori_loop(0, i * (kpq // 2), two_blocks, 0)
    d0 = i * kpq
    scores(d0 + 1, 1)
    consume(d0, 0, 0)
    consume(d0 + 1, 1, 1)
    o_ref[...] = (acc_ref[...] / l_ref[...]).T.astype(BF16)


def _flash(qT, k, vT, bsz, seq):
    bq, bk = FLASH_Q_BLOCK, ATTN_BLOCK
    assert bq == 2 * bk and seq % bq == 0
    return pl.pallas_call(
        _flash_kernel,
        grid=(bsz, MLA_HEADS, seq // bq),
        in_specs=[
            pl.BlockSpec((None, None, MLA_QK_PAD, bq), lambda b, h, i: (b, h, 0, i)),
            pl.BlockSpec((None, None, seq, MLA_QK_PAD), lambda b, h, i: (b, h, 0, 0)),
            pl.BlockSpec((None, None, seq // bk, MLA_V, bk), lambda b, h, i: (b, h, 0, 0, 0)),
        ],
        out_specs=pl.BlockSpec((None, bq, MLA_V), lambda b, h, i: (b, i, h)),
        out_shape=jax.ShapeDtypeStruct((bsz, seq, MLA_HEADS * MLA_V), BF16),
        scratch_shapes=[
            pltpu.VMEM((2, bk, bq), F32),
            pltpu.VMEM((2, 1, bq), F32),
            pltpu.VMEM((1, bq), F32),
            pltpu.VMEM((1, bq), F32),
            pltpu.VMEM((MLA_V, bq), F32),
        ],
        compiler_params=_params(3),
        name="flash",
    )(qT, k, vT)


def _softplus(x):
    return jnp.maximum(x, 0.0) + jnp.log1p(jnp.exp(-jnp.abs(x)))


def _ssd_kernel(xc_ref, dt_ref, dtT_ref, e4_ref, dtb_ref, dtbT_ref, alog_ref, alogT_ref, dskip_ref,
                y_ref, state_ref):
    step_len = xc_ref.shape[0]
    cl = SSD_CHUNK

    @pl.when(pl.program_id(1) == 0)
    def _():
        state_ref[...] = jnp.zeros(state_ref.shape, F32)

    dt_all = _softplus(dt_ref[...] + dtb_ref[...])
    dtT_all = _softplus(dtT_ref[...] + dtbT_ref[...])
    a_row = -jnp.exp(alog_ref[...])
    a_col = -jnp.exp(alogT_ref[...])

    row = lax.broadcasted_iota(jnp.int32, (cl, cl), 0)
    col = lax.broadcasted_iota(jnp.int32, (cl, cl), 1)
    causal = col <= row
    tril = causal.astype(F32)
    triu = (row <= col).astype(F32)
    lane = lax.broadcasted_iota(jnp.int32, (cl, LANES), 1)
    left_head = lane < SSD_HEAD_DIM
    odd_copy = (lane & SSD_HEADS) != 0

    for c in range(step_len // cl):
        r0 = c * cl
        xs_b = xc_ref[r0:r0 + cl, :SSD_D_INNER]
        xs = xs_b.astype(F32)
        bm = xc_ref[r0:r0 + cl, SSD_D_INNER:SSD_D_INNER + SSD_BC]
        cm = xc_ref[r0:r0 + cl, SSD_D_INNER + SSD_BC:]
        dt = dt_all[r0:r0 + cl]
        dtT = dtT_all[:, r0:r0 + cl]
        acum = jnp.dot(tril, dt * a_row, preferred_element_type=F32, precision=lax.Precision.HIGHEST)
        acumT = jnp.dot(dtT * a_col, triu, preferred_element_type=F32, precision=lax.Precision.HIGHEST)
        last = acum[cl - 1:cl, :]
        fac = jnp.where(odd_copy, dt * jnp.exp(last - acum), jnp.exp(acum))
        hi = fac.astype(BF16).astype(F32)
        split = jnp.where(lane < 2 * SSD_HEADS, hi, fac - hi).astype(BF16)
        spread = jnp.dot(split, e4_ref[...], preferred_element_type=F32)
        y_off_scale = spread[:, :SSD_D_INNER]
        xw = (xs * spread[:, SSD_D_INNER:]).astype(BF16)
        state_decay = y_off_scale[cl - 1:cl, :]

        y_parts = []
        for g in range(SSD_GROUPS):
            bg = bm[:, g * SSD_STATE:(g + 1) * SSD_STATE]
            cg = cm[:, g * SSD_STATE:(g + 1) * SSD_STATE]
            gsl = slice(g * SSD_HPG * SSD_HEAD_DIM, (g + 1) * SSD_HPG * SSD_HEAD_DIM)
            cb = lax.dot_general(cg, bg, NT_DIMS, preferred_element_type=F32)
            st = state_ref[:, gsl]
            y_off = jnp.dot(cg, st.astype(BF16), preferred_element_type=F32) * y_off_scale[:, gsl]
            for j in range(SSD_HPG // 2):
                mats = []
                for h in (g * SSD_HPG + 2 * j, g * SSD_HPG + 2 * j + 1):
                    seg = acum[:, h:h + 1] - acumT[h:h + 1, :]
                    decay = jnp.exp(jnp.where(causal, seg, -jnp.inf))
                    mats.append(cb * decay * dtT[h:h + 1, :])
                lhs = jnp.concatenate(mats, axis=1).astype(BF16)
                pair = slice((g * SSD_HPG + 2 * j) * SSD_HEAD_DIM, (g * SSD_HPG + 2 * j + 2) * SSD_HEAD_DIM)
                xp = xs_b[:, pair]
                zero = jnp.zeros_like(xp)
                rhs = jnp.concatenate([jnp.where(left_head, xp, zero), jnp.where(left_head, zero, xp)],
                                      axis=0)
                y_diag = jnp.dot(lhs, rhs, preferred_element_type=F32)
                lsl = slice(2 * j * SSD_HEAD_DIM, (2 * j + 2) * SSD_HEAD_DIM)
                y_parts.append(y_diag + y_off[:, lsl])
            state_ref[:, gsl] = st * state_decay[:, gsl] + lax.dot_general(
                bg, xw[:, gsl], TN_DIMS, preferred_element_type=F32)
        y = jnp.concatenate(y_parts, axis=1) + xs * dskip_ref[...]
        y_ref[r0:r0 + cl, :] = y.astype(BF16)


def _ssd(xc, dt, dtT, w, bsz, seq):
    ls = SSD_STEP
    consts = (w["head_spread"], w["dt_bias4"], w["dt_bias_col"], w["a_log4"], w["a_log_col"], w["d_skip_x"])
    return pl.pallas_call(
        _ssd_kernel,
        grid=(bsz, seq // ls),
        in_specs=[
            pl.BlockSpec((None, ls, SSD_CONV_DIM), lambda b, i: (b, i, 0)),
            pl.BlockSpec((None, ls, LANES), lambda b, i: (b, i, 0)),
            pl.BlockSpec((None, SSD_HEADS, ls), lambda b, i: (b, 0, i)),
        ] + [_resident(x.shape) for x in consts],
        out_specs=pl.BlockSpec((None, ls, SSD_D_INNER), lambda b, i: (b, i, 0)),
        out_shape=jax.ShapeDtypeStruct((bsz, seq, SSD_D_INNER), BF16),
        scratch_shapes=[pltpu.VMEM((SSD_STATE, SSD_D_INNER), F32)],
        compiler_params=_params(2),
        name="ssd",
    )(xc, dt, dtT, *consts)


def _merge_kernel(h_ref, y_ref, zs_ref, o_ref, gate_ref, gssd_ref, wssd_ref, wmla_ref, wo_ref, out_ref):
    yg = y_ref[...].astype(F32) * zs_ref[...].astype(F32)
    y_ssd = jnp.dot(_rms(yg, gssd_ref[...]).astype(BF16), wssd_ref[...], preferred_element_type=F32)
    y_mla = jnp.dot(o_ref[...], wmla_ref[...], preferred_element_type=F32)
    gate = gate_ref[...].astype(F32)
    merged = gate[:, :D_MODEL] * y_ssd + gate[:, D_MODEL:] * y_mla
    out_ref[...] = h_ref[...] + jnp.dot(merged.astype(BF16), wo_ref[...], preferred_element_type=F32)


def _merge(h, y, zs, o, gate, w):
    t = h.shape[0]
    tm = TOKEN_TILE
    consts = (w["g_ssd"], w["w_ssd_out"], w["w_mla_out"], w["w_o"])

    def rows(width):
        return pl.BlockSpec((tm, width), lambda i: (i, 0))

    return pl.pallas_call(
        _merge_kernel,
        grid=(t // tm,),
        in_specs=[rows(D_MODEL), rows(SSD_D_INNER), rows(SSD_D_INNER), rows(MLA_HEADS * MLA_V), rows(2 * D_MODEL)]
        + [_resident(x.shape) for x in consts],
        out_specs=rows(D_MODEL),
        out_shape=jax.ShapeDtypeStruct((t, D_MODEL), F32),
        compiler_params=_params(1),
        name="merge",
    )(h, y, zs, o, gate, *consts)


def _split_cols(t, sizes):
    out, start = [], 0
    for n in sizes:
        out.append(t[..., start:start + n])
        start += n
    return out


def _head_spread_matrix():
    src = np.arange(LANES)
    dst = np.arange(2 * SSD_D_INNER)
    same_head = (src[:, None] % SSD_HEADS) == (dst[None, :] % SSD_D_INNER) // SSD_HEAD_DIM
    same_half = ((src[:, None] // SSD_HEADS) % 2) == dst[None, :] // SSD_D_INNER
    return jnp.asarray(same_head & same_half, dtype=BF16)


def _layer_weights(l, p):
    half = MLA_ROPE // 2
    w_z, w_xbc, w_dt, w_cq, w_ckv, w_kr, w_gate = _split_cols(
        p["w_in"][l], (SSD_D_INNER, SSD_CONV_DIM, SSD_HEADS, MLA_Q_LORA, MLA_KV_LORA, MLA_ROPE, 2 * D_MODEL))
    w_krx = jnp.concatenate([w_kr, w_kr[:, half:], w_kr[:, :half]], axis=1)
    w_ukv = p["w_ukv"][l].reshape(MLA_KV_LORA, MLA_HEADS, MLA_NOPE + MLA_V)
    w_uk = w_ukv[:, :, :MLA_NOPE].reshape(MLA_KV_LORA, MLA_HEADS * MLA_NOPE)
    w_uv = w_ukv[:, :, MLA_NOPE:].reshape(MLA_KV_LORA, MLA_HEADS * MLA_V)
    g_q = p["q_norm"][l] * float(MLA_QK ** -0.5 * np.log2(np.e))
    g_k = p["k_norm"][l]
    g1, g2 = g_k[MLA_NOPE:MLA_NOPE + half], g_k[MLA_NOPE + half:]
    rope_inv = 1.0 / (ROPE_THETA ** (jnp.arange(0, MLA_ROPE, 2, dtype=F32) / MLA_ROPE))
    return {
        "g_ffn1": p["ln_ffn1"][l][None, :],
        "ffn1_w13": p["ffn1_w13"][l].astype(BF16),
        "ffn1_w2": p["ffn1_w2"][l].astype(BF16),
        "g_ffn2": p["ln_ffn2"][l][None, :],
        "ffn2_w13": p["ffn2_w13"][l].astype(BF16),
        "ffn2_w2": p["ffn2_w2"][l].astype(BF16),
        "g_mix": p["ln_mix"][l][None, :],
        "w_z": w_z.astype(BF16),
        "w_xbc": w_xbc.astype(BF16),
        "w_dt4": jnp.tile(w_dt, (1, LANES // SSD_HEADS)).astype(BF16),
        "w_dtT": w_dt.T.astype(BF16),
        "w_cq": w_cq.astype(BF16),
        "w_ckv": w_ckv.astype(BF16),
        "w_krx": w_krx.astype(BF16),
        "w_gate": w_gate.astype(BF16),
        "conv_w": p["conv_w"][l],
        "conv_b": p["conv_b"][l][None, :],
        "dt_bias4": jnp.tile(p["dt_bias"][l], LANES // SSD_HEADS)[None, :],
        "dt_bias_col": p["dt_bias"][l][:, None],
        "a_log4": jnp.tile(p["a_log"][l], LANES // SSD_HEADS)[None, :],
        "head_spread": _head_spread_matrix(),
        "a_log_col": p["a_log"][l][:, None],
        "d_skip_x": jnp.repeat(p["d_skip"][l], SSD_HEAD_DIM)[None, :],
        "g_ssd": p["ssd_norm"][l][None, :],
        "w_ssd_out": p["w_ssd_out"][l].astype(BF16),
        "rope_inv": rope_inv[:, None],
        "g_q_lora": p["q_lora_norm"][l][None, :],
        "w_uqT": p["w_uq"][l].T.astype(BF16),
        "g_kv_lora": p["kv_lora_norm"][l][None, :],
        "w_uk": w_uk.astype(BF16),
        "w_uvT": w_uv.T.astype(BF16),
        "g_q_col": g_q[:, None],
        "g_k_nope": g_k[None, :MLA_NOPE],
        "g_k_rope_col": jnp.concatenate([g1, g2, -g2, g1])[:, None],
        "w_mla_out": p["w_mla_out"][l].astype(BF16),
        "w_o": p["w_o"][l].astype(BF16),
    }


def kernel(x, positions, ln_ffn1, ffn1_w13, ffn1_w2, ln_mix, w_in, conv_w, conv_b, dt_bias, a_log, d_skip,
           ssd_norm, w_ssd_out, q_lora_norm, w_uq, kv_lora_norm, w_ukv, q_norm, k_norm, w_mla_out, w_o,
           ln_ffn2, ffn2_w13, ffn2_w2):
    bsz, seq, d_model = x.shape
    assert d_model == D_MODEL and seq % FLASH_Q_BLOCK == 0 and seq % SSD_STEP == 0 and seq % TOKEN_TILE == 0
    params = dict(ln_ffn1=ln_ffn1, ffn1_w13=ffn1_w13, ffn1_w2=ffn1_w2, ln_mix=ln_mix, w_in=w_in, conv_w=conv_w,
                  conv_b=conv_b, dt_bias=dt_bias, a_log=a_log, d_skip=d_skip, ssd_norm=ssd_norm,
                  w_ssd_out=w_ssd_out, q_lora_norm=q_lora_norm, w_uq=w_uq, kv_lora_norm=kv_lora_norm, w_ukv=w_ukv,
                  q_norm=q_norm, k_norm=k_norm, w_mla_out=w_mla_out, w_o=w_o, ln_ffn2=ln_ffn2, ffn2_w13=ffn2_w13,
                  ffn2_w2=ffn2_w2)
    tokens = bsz * seq
    pos_row = positions.reshape(bsz, 1, seq)
    h = x.reshape(tokens, D_MODEL)
    for l in range(ln_ffn1.shape[0]):
        w = _layer_weights(l, params)
        h = _ffn(h, w["g_ffn1"], w["ffn1_w13"], w["ffn1_w2"])
        zs, xc, dt, dtT, cq, ckv, krx, gate = _in_proj(h, w["g_mix"], w, bsz, seq)
        y = _ssd(xc, dt, dtT, w, bsz, seq)
        qT, k, vT = _mla_prep(cq, ckv, krx, pos_row, w, bsz, seq)
        o = _flash(qT, k, vT, bsz, seq)
        h = _merge(h, y.reshape(tokens, SSD_D_INNER), zs.reshape(tokens, SSD_D_INNER),
                   o.reshape(tokens, MLA_HEADS * MLA_V), gate.reshape(tokens, 2 * D_MODEL), w)
        h = _ffn(h, w["g_ffn2"], w["ffn2_w13"], w["ffn2_w2"])
    return h.reshape(bsz, seq, D_MODEL)
```

```python
import jax
import jax.numpy as jnp
import numpy as np
from jax import lax
from jax.experimental import pallas as pl
from jax.experimental.pallas import tpu as pltpu

F32 = jnp.float32
BF16 = jnp.bfloat16

D_MODEL = 1024
D_FF = 2816
SSD_D_INNER = 2048
SSD_HEAD_DIM = 64
SSD_HEADS = 32
SSD_GROUPS = 4
SSD_HPG = 8
SSD_STATE = 128
SSD_CONV = 4
SSD_CHUNK = 128
SSD_BC = SSD_GROUPS * SSD_STATE
SSD_CONV_DIM = SSD_D_INNER + 2 * SSD_BC
MLA_HEADS = 8
MLA_Q_LORA = 512
MLA_KV_LORA = 256
MLA_NOPE = 128
MLA_ROPE = 64
MLA_V = 128
MLA_QK = MLA_NOPE + MLA_ROPE
MLA_QK_PAD = 256
ROPE_THETA = 10000.0
EPS = 1e-6

LANES = 128
VMEM_LIMIT_BYTES = 56 * 1024 * 1024

TOKEN_TILE = 512
IN_PROJ_TILE = 256
SSD_STEP = 256
ATTN_BLOCK = 512
FLASH_Q_BLOCK = 1024
CONV_HALO = 8

NT_DIMS = (((1,), (1,)), ((), ()))
TN_DIMS = (((0,), (0,)), ((), ()))


def _params(n_grid_axes):
    return pltpu.CompilerParams(
        dimension_semantics=("arbitrary",) * n_grid_axes,
        vmem_limit_bytes=VMEM_LIMIT_BYTES,
    )


def _resident(shape):
    zeros = (0,) * len(shape)
    return pl.BlockSpec(shape, lambda *_: zeros, pipeline_mode=pl.Buffered(1))


def _rms(x, gain):
    return x * lax.rsqrt(jnp.mean(x * x, axis=-1, keepdims=True) + EPS) * gain


def _silu(x):
    return x * jax.nn.sigmoid(x)


def _ffn_kernel(h_ref, g_ref, w13_ref, w2_ref, o_ref):
    x = h_ref[...]
    xb = _rms(x, g_ref[...]).astype(BF16)
    gate = jnp.dot(xb, w13_ref[:, :D_FF], preferred_element_type=F32)
    up = jnp.dot(xb, w13_ref[:, D_FF:], preferred_element_type=F32)
    hidden = (_silu(gate) * up).astype(BF16)
    o_ref[...] = x + 0.5 * jnp.dot(hidden, w2_ref[...], preferred_element_type=F32)


def _ffn(h, gain, w13, w2):
    t = h.shape[0]
    return pl.pallas_call(
        _ffn_kernel,
        grid=(t // TOKEN_TILE,),
        in_specs=[
            pl.BlockSpec((TOKEN_TILE, D_MODEL), lambda i: (i, 0)),
            _resident((1, D_MODEL)),
            _resident((D_MODEL, 2 * D_FF)),
            _resident((D_FF, D_MODEL)),
        ],
        out_specs=pl.BlockSpec((TOKEN_TILE, D_MODEL), lambda i: (i, 0)),
        out_shape=jax.ShapeDtypeStruct((t, D_MODEL), F32),
        compiler_params=_params(1),
        name="ffn",
    )(h, gain, w13, w2)


def _in_proj_kernel(h_ref, g_ref, wz_ref, wxbc_ref, wdt_ref, wdtT_ref, wcq_ref, wckv_ref, wkrx_ref, wg_ref,
                    zs_ref, xbc_ref, dt_ref, dtT_ref, cq_ref, ckv_ref, krx_ref, gate_ref):
    xb = _rms(h_ref[...], g_ref[...]).astype(BF16)

    def proj(w_ref):
        return jnp.dot(xb, w_ref[...], preferred_element_type=F32)

    zs_ref[...] = _silu(proj(wz_ref)).astype(BF16)
    gate_ref[...] = jax.nn.sigmoid(proj(wg_ref)).astype(BF16)

    xbc_ref[...] = proj(wxbc_ref).astype(BF16)
    dt_ref[...] = proj(wdt_ref)
    dtT_ref[...] = lax.dot_general(wdtT_ref[...], xb, NT_DIMS, preferred_element_type=F32)
    cq_ref[...] = proj(wcq_ref).astype(BF16)
    ckv_ref[...] = proj(wckv_ref).astype(BF16)
    krx_ref[...] = proj(wkrx_ref)


def _in_proj(h, gain, w, bsz, seq):
    tm = IN_PROJ_TILE
    nt = seq // tm
    widths = (SSD_D_INNER, SSD_CONV_DIM, LANES, None, MLA_Q_LORA, MLA_KV_LORA, LANES, 2 * D_MODEL)
    dtypes = (BF16, BF16, F32, F32, BF16, BF16, F32, BF16)
    out_specs, out_shape = [], []
    for width, dtype in zip(widths, dtypes):
        if width is None:
            out_specs.append(pl.BlockSpec((None, SSD_HEADS, tm), lambda b, i: (b, 0, i)))
            out_shape.append(jax.ShapeDtypeStruct((bsz, SSD_HEADS, seq), dtype))
        else:
            out_specs.append(pl.BlockSpec((None, tm, width), lambda b, i: (b, i, 0)))
            out_shape.append(jax.ShapeDtypeStruct((bsz, seq, width), dtype))
    weights = (w["w_z"], w["w_xbc"], w["w_dt4"], w["w_dtT"], w["w_cq"], w["w_ckv"], w["w_krx"], w["w_gate"])
    return pl.pallas_call(
        _in_proj_kernel,
        grid=(bsz, nt),
        in_specs=[pl.BlockSpec((None, tm, D_MODEL), lambda b, i: (b, i, 0)), _resident((1, D_MODEL))]
        + [_resident(x.shape) for x in weights],
        out_specs=out_specs,
        out_shape=out_shape,
        compiler_params=_params(2),
        name="in_proj",
    )(h.reshape(bsz, seq, D_MODEL), gain, *weights)


def _mla_prep_kernel(cq_ref, ckv_ref, krx_ref, pos_ref, inv_ref, gql_ref, wuqT_ref, gkvl_ref, wuk_ref, wuvT_ref,
                     gq_ref, gkn_ref, gkr_ref, qT_ref, k_ref, vT_ref):
    tm = cq_ref.shape[0]
    half = MLA_ROPE // 2
    ang = inv_ref[...] * pos_ref[...].astype(F32)
    cos, sin = jnp.cos(ang), jnp.sin(ang)

    qn = _rms(cq_ref[...].astype(F32), gql_ref[...]).astype(BF16)
    qT = lax.dot_general(wuqT_ref[...], qn, NT_DIMS, preferred_element_type=F32)
    gq = jnp.broadcast_to(gq_ref[...], (MLA_QK, tm))
    for h in range(MLA_HEADS):
        blk = qT[h * MLA_QK:(h + 1) * MLA_QK]
        inv_rms = lax.rsqrt(jnp.sum(blk * blk, axis=0, keepdims=True) * (1.0 / MLA_QK) + EPS)
        xn = blk * inv_rms * gq
        t1 = xn[MLA_NOPE:MLA_NOPE + half]
        t2 = xn[MLA_NOPE + half:]
        qT_ref[h, 0:MLA_NOPE, :] = xn[:MLA_NOPE].astype(BF16)
        qT_ref[h, MLA_NOPE:MLA_NOPE + half, :] = (t1 * cos - t2 * sin).astype(BF16)
        qT_ref[h, MLA_NOPE + half:MLA_QK, :] = (t2 * cos + t1 * sin).astype(BF16)
        qT_ref[h, MLA_QK:, :] = jnp.zeros((MLA_QK_PAD - MLA_QK, tm), BF16)

    kvn = _rms(ckv_ref[...].astype(F32), gkvl_ref[...]).astype(BF16)
    kn = jnp.dot(kvn, wuk_ref[...], preferred_element_type=F32)
    vT = lax.dot_general(wuvT_ref[...], kvn, NT_DIMS, preferred_element_type=F32)
    vT_ref[...] = vT.reshape(MLA_HEADS, MLA_V, tm).astype(BF16)

    krx = krx_ref[...]
    rope_ss = 0.5 * jnp.sum(krx * krx, axis=-1, keepdims=True)
    table = (jnp.concatenate([cos, cos, sin, sin], axis=0) * gkr_ref[...]).T
    prod = krx * table
    k_rope = prod + pltpu.roll(prod, MLA_ROPE, 1)
    gkn = gkn_ref[...]
    for h in range(MLA_HEADS):
        knh = kn[:, h * MLA_NOPE:(h + 1) * MLA_NOPE]
        ss = jnp.sum(knh * knh, axis=-1, keepdims=True) + rope_ss
        inv_rms = lax.rsqrt(ss * (1.0 / MLA_QK) + EPS)
        k_ref[h, :, 0:MLA_NOPE] = (knh * inv_rms * gkn).astype(BF16)
        k_ref[h, :, MLA_NOPE:] = (k_rope * inv_rms).astype(BF16)


def _mla_prep(cq, ckv, krx, pos_row, w, bsz, seq):
    tm = ATTN_BLOCK
    nt = seq // tm
    consts = (w["rope_inv"], w["g_q_lora"], w["w_uqT"], w["g_kv_lora"], w["w_uk"], w["w_uvT"],
              w["g_q_col"], w["g_k_nope"], w["g_k_rope_col"])
    in_specs = [
        pl.BlockSpec((None, tm, MLA_Q_LORA), lambda b, i: (b, i, 0)),
        pl.BlockSpec((None, tm, MLA_KV_LORA), lambda b, i: (b, i, 0)),
        pl.BlockSpec((None, tm, LANES), lambda b, i: (b, i, 0)),
        pl.BlockSpec((None, 1, tm), lambda b, i: (b, 0, i)),
    ] + [_resident(x.shape) for x in consts]
    out_specs = [
        pl.BlockSpec((None, MLA_HEADS, MLA_QK_PAD, tm), lambda b, i: (b, 0, 0, i)),
        pl.BlockSpec((None, MLA_HEADS, tm, MLA_QK_PAD), lambda b, i: (b, 0, i, 0)),
        pl.BlockSpec((None, MLA_HEADS, None, MLA_V, tm), lambda b, i: (b, 0, i, 0, 0)),
    ]
    out_shape = [
        jax.ShapeDtypeStruct((bsz, MLA_HEADS, MLA_QK_PAD, seq), BF16),
        jax.ShapeDtypeStruct((bsz, MLA_HEADS, seq, MLA_QK_PAD), BF16),
        jax.ShapeDtypeStruct((bsz, MLA_HEADS, nt, MLA_V, tm), BF16),
    ]
    return pl.pallas_call(
        _mla_prep_kernel,
        grid=(bsz, nt),
        in_specs=in_specs,
        out_specs=out_specs,
        out_shape=out_shape,
        compiler_params=_params(2),
        name="mla_prep",
    )(cq, ckv, krx, pos_row, *consts)


def _flash_kernel(qT_ref, qnext_ref, k_ref, vT_ref, o_ref, s_ref, cmax_ref, m_ref, l_ref, acc_ref):
    bq, bk = FLASH_Q_BLOCK, ATTN_BLOCK
    kpq = bq // bk
    i = pl.program_id(2)
    m_ref[...] = jnp.full(m_ref.shape, -jnp.inf, F32)
    l_ref[...] = jnp.zeros(l_ref.shape, F32)
    acc_ref[...] = jnp.zeros(acc_ref.shape, F32)

    def scores(j, slot, q_ref=qT_ref):
        k_blk = k_ref[pl.ds(pl.multiple_of(j * bk, bk), bk), :]
        sT = jnp.dot(k_blk, q_ref[...], preferred_element_type=F32)
        s_ref[slot] = sT
        cmax_ref[slot] = jnp.max(sT, axis=0, keepdims=True)

    def softmax_pv(j, sT, cmax, qs):
        m_prev = m_ref[:, qs]
        m_new = jnp.maximum(m_prev, cmax)
        alpha = jnp.exp2(m_prev - m_new)
        p = jnp.exp2(sT - m_new)
        l_ref[:, qs] = alpha * l_ref[:, qs] + jnp.sum(p, axis=0, keepdims=True)
        pv = jnp.dot(vT_ref[j], p.astype(BF16), preferred_element_type=F32)
        acc_ref[:, qs] = alpha * acc_ref[:, qs] + pv
        m_ref[:, qs] = m_new

    def consume(j, slot):
        softmax_pv(j, s_ref[slot], cmax_ref[slot], slice(None))

    def two_blocks(j):
        scores(j + 1, 1)
        consume(j, 0)
        scores(j + 2, 0)
        consume(j + 1, 1)

    def four_blocks(t, carry):
        two_blocks(4 * t)
        two_blocks(4 * t + 2)
        return carry

    @pl.when(i == 0)
    def _():
        scores(0, 0)

    fours = lax.shift_right_logical(i, 1)
    lax.fori_loop(0, fours, four_blocks, 0)

    @pl.when((i & 1) == 1)
    def _():
        two_blocks(4 * fours)

    d0 = i * kpq
    upper = slice(bk, bq)
    k_hi = k_ref[pl.ds(pl.multiple_of((d0 + 1) * bk, bk), bk), :]
    s_hi = jnp.dot(k_hi, qT_ref[:, upper], preferred_element_type=F32)
    kpos = lax.broadcasted_iota(jnp.int32, (bk, bk), 0)
    qpos = lax.broadcasted_iota(jnp.int32, (bk, bk), 1)
    tri = kpos <= qpos
    s_lo = s_ref[0]
    s_lo = jnp.concatenate([jnp.where(tri, s_lo[:, :bk], -jnp.inf), s_lo[:, upper]], axis=1)
    softmax_pv(d0, s_lo, jnp.max(s_lo, axis=0, keepdims=True), slice(None))
    scores(0, 0, qnext_ref)
    s_hi = jnp.where(tri, s_hi, -jnp.inf)
    softmax_pv(d0 + 1, s_hi, jnp.max(s_hi, axis=0, keepdims=True), upper)
    o_ref[...] = (acc_ref[...] / l_ref[...]).T.astype(BF16)


def _flash(qT, k, vT, bsz, seq):
    bq, bk = FLASH_Q_BLOCK, ATTN_BLOCK
    assert bq == 2 * bk and seq % bq == 0
    nq = seq // bq
    return pl.pallas_call(
        _flash_kernel,
        grid=(bsz, MLA_HEADS, nq),
        in_specs=[
            pl.BlockSpec((None, None, MLA_QK_PAD, bq), lambda b, h, i: (b, h, 0, i)),
            pl.BlockSpec((None, None, MLA_QK_PAD, bq), lambda b, h, i: (b, h, 0, jnp.minimum(i + 1, nq - 1))),
            pl.BlockSpec((None, None, seq, MLA_QK_PAD), lambda b, h, i: (b, h, 0, 0)),
            pl.BlockSpec((None, None, seq // bk, MLA_V, bk), lambda b, h, i: (b, h, 0, 0, 0)),
        ],
        out_specs=pl.BlockSpec((None, bq, MLA_V), lambda b, h, i: (b, i, h)),
        out_shape=jax.ShapeDtypeStruct((bsz, seq, MLA_HEADS * MLA_V), BF16),
        scratch_shapes=[
            pltpu.VMEM((2, bk, bq), F32),
            pltpu.VMEM((2, 1, bq), F32),
            pltpu.VMEM((1, bq), F32),
            pltpu.VMEM((1, bq), F32),
            pltpu.VMEM((MLA_V, bq), F32),
        ],
        compiler_params=_params(3),
        name="flash",
    )(qT, qT, k, vT)


def _softplus(x):
    return jnp.maximum(x, 0.0) + jnp.log1p(jnp.exp(-jnp.abs(x)))


def _ssd_kernel(xbc_ref, dt_ref, dtT_ref, convw_ref, convb_ref, shift_ref, e4_ref, dtb_ref, dtbT_ref, alog_ref,
                alogT_ref, dskip_ref, y_ref, state_ref, halo_ref):
    step_len = xbc_ref.shape[0]
    cl = SSD_CHUNK

    @pl.when(pl.program_id(1) == 0)
    def _():
        state_ref[...] = jnp.zeros(state_ref.shape, F32)
        halo_ref[...] = jnp.zeros(halo_ref.shape, F32)

    xh = xbc_ref[...]
    xf = xh.astype(F32)
    conv = convb_ref[...] + convw_ref[SSD_CONV - 1:SSD_CONV, :] * xf
    for k in range(SSD_CONV - 1):
        conv = conv + convw_ref[k:k + 1, :] * jnp.dot(shift_ref[k], xh, preferred_element_type=F32)
    head_in = jnp.concatenate([halo_ref[...], xf[:CONV_HALO]], axis=0)
    head = convb_ref[...] + convw_ref[SSD_CONV - 1:SSD_CONV, :] * xf[:CONV_HALO]
    for k in range(SSD_CONV - 1):
        start = CONV_HALO - (SSD_CONV - 1) + k
        head = head + convw_ref[k:k + 1, :] * head_in[start:start + CONV_HALO]
    halo_ref[...] = xf[step_len - CONV_HALO:]
    xc = _silu(jnp.concatenate([head, conv[CONV_HALO:]], axis=0))

    dt_all = _softplus(dt_ref[...] + dtb_ref[...])
    dtT_all = _softplus(dtT_ref[...] + dtbT_ref[...])
    a_row = -jnp.exp(alog_ref[...])
    a_col = -jnp.exp(alogT_ref[...])

    row = lax.broadcasted_iota(jnp.int32, (cl, cl), 0)
    col = lax.broadcasted_iota(jnp.int32, (cl, cl), 1)
    causal = col <= row
    tril = causal.astype(F32)
    triu = (row <= col).astype(F32)
    lane = lax.broadcasted_iota(jnp.int32, (cl, LANES), 1)
    left_head = lane < SSD_HEAD_DIM
    odd_copy = (lane & SSD_HEADS) != 0

    for c in range(step_len // cl):
        r0 = c * cl
        xs = xc[r0:r0 + cl, :SSD_D_INNER]
        xs_b = xs.astype(BF16)
        bm = xc[r0:r0 + cl, SSD_D_INNER:SSD_D_INNER + SSD_BC].astype(BF16)
        cm = xc[r0:r0 + cl, SSD_D_INNER + SSD_BC:].astype(BF16)
        dt = dt_all[r0:r0 + cl]
        dtT = dtT_all[:, r0:r0 + cl]
        acum = jnp.dot(tril, dt * a_row, preferred_element_type=F32, precision=lax.Precision.HIGHEST)
        acumT = jnp.dot(dtT * a_col, triu, preferred_element_type=F32, precision=lax.Precision.HIGHEST)
        last = acum[cl - 1:cl, :]
        fac = jnp.where(odd_copy, dt * jnp.exp(last - acum), jnp.exp(acum))
        hi = fac.astype(BF16).astype(F32)
        split = jnp.where(lane < 2 * SSD_HEADS, hi, fac - hi).astype(BF16)
        spread = jnp.dot(split, e4_ref[...], preferred_element_type=F32)
        y_off_scale = spread[:, :SSD_D_INNER]
        xw = (xs * spread[:, SSD_D_INNER:]).astype(BF16)
        state_decay = y_off_scale[cl - 1:cl, :]

        y_parts = []
        for g in range(SSD_GROUPS):
            bg = bm[:, g * SSD_STATE:(g + 1) * SSD_STATE]
            cg = cm[:, g * SSD_STATE:(g + 1) * SSD_STATE]
            gsl = slice(g * SSD_HPG * SSD_HEAD_DIM, (g + 1) * SSD_HPG * SSD_HEAD_DIM)
            cb = lax.dot_general(cg, bg, NT_DIMS, preferred_element_type=F32)
            st = state_ref[:, gsl]
            y_off = jnp.dot(cg, st.astype(BF16), preferred_element_type=F32) * y_off_scale[:, gsl]
            for j in range(SSD_HPG // 2):
                mats = []
                for h in (g * SSD_HPG + 2 * j, g * SSD_HPG + 2 * j + 1):
                    seg = acum[:, h:h + 1] - acumT[h:h + 1, :]
                    decay = jnp.exp(jnp.where(causal, seg, -jnp.inf))
                    mats.append(cb * decay * dtT[h:h + 1, :])
                lhs = jnp.concatenate(mats, axis=1).astype(BF16)
                pair = slice((g * SSD_HPG + 2 * j) * SSD_HEAD_DIM, (g * SSD_HPG + 2 * j + 2) * SSD_HEAD_DIM)
                xp = xs_b[:, pair]
                zero = jnp.zeros_like(xp)
                rhs = jnp.concatenate([jnp.where(left_head, xp, zero), jnp.where(left_head, zero, xp)],
                                      axis=0)
                y_diag = jnp.dot(lhs, rhs, preferred_element_type=F32)
                lsl = slice(2 * j * SSD_HEAD_DIM, (2 * j + 2) * SSD_HEAD_DIM)
                y_parts.append(y_diag + y_off[:, lsl])
            state_ref[:, gsl] = st * state_decay[:, gsl] + lax.dot_general(
                bg, xw[:, gsl], TN_DIMS, preferred_element_type=F32)
        y = jnp.concatenate(y_parts, axis=1) + xs * dskip_ref[...]
        y_ref[r0:r0 + cl, :] = y.astype(BF16)


def _ssd(xbc, dt, dtT, w, bsz, seq):
    ls = SSD_STEP
    consts = (w["conv_w"], w["conv_b"], w["conv_shift"], w["head_spread"], w["dt_bias4"], w["dt_bias_col"],
              w["a_log4"], w["a_log_col"], w["d_skip_x"])
    return pl.pallas_call(
        _ssd_kernel,
        grid=(bsz, seq // ls),
        in_specs=[
            pl.BlockSpec((None, ls, SSD_CONV_DIM), lambda b, i: (b, i, 0)),
            pl.BlockSpec((None, ls, LANES), lambda b, i: (b, i, 0)),
            pl.BlockSpec((None, SSD_HEADS, ls), lambda b, i: (b, 0, i)),
        ] + [_resident(x.shape) for x in consts],
        out_specs=pl.BlockSpec((None, ls, SSD_D_INNER), lambda b, i: (b, i, 0)),
        out_shape=jax.ShapeDtypeStruct((bsz, seq, SSD_D_INNER), BF16),
        scratch_shapes=[pltpu.VMEM((SSD_STATE, SSD_D_INNER), F32), pltpu.VMEM((CONV_HALO, SSD_CONV_DIM), F32)],
        compiler_params=_params(2),
        name="ssd",
    )(xbc, dt, dtT, *consts)


def _merge_kernel(h_ref, y_ref, zs_ref, o_ref, gate_ref, gssd_ref, wssd_ref, wmla_ref, wo_ref, out_ref):
    yg = y_ref[...].astype(F32) * zs_ref[...].astype(F32)
    y_ssd = jnp.dot(_rms(yg, gssd_ref[...]).astype(BF16), wssd_ref[...], preferred_element_type=F32)
    y_mla = jnp.dot(o_ref[...], wmla_ref[...], preferred_element_type=F32)
    gate = gate_ref[...].astype(F32)
    merged = gate[:, :D_MODEL] * y_ssd + gate[:, D_MODEL:] * y_mla
    out_ref[...] = h_ref[...] + jnp.dot(merged.astype(BF16), wo_ref[...], preferred_element_type=F32)


def _merge(h, y, zs, o, gate, w):
    t = h.shape[0]
    tm = TOKEN_TILE
    consts = (w["g_ssd"], w["w_ssd_out"], w["w_mla_out"], w["w_o"])

    def rows(width):
        return pl.BlockSpec((tm, width), lambda i: (i, 0))

    return pl.pallas_call(
        _merge_kernel,
        grid=(t // tm,),
        in_specs=[rows(D_MODEL), rows(SSD_D_INNER), rows(SSD_D_INNER), rows(MLA_HEADS * MLA_V), rows(2 * D_MODEL)]
        + [_resident(x.shape) for x in consts],
        out_specs=rows(D_MODEL),
        out_shape=jax.ShapeDtypeStruct((t, D_MODEL), F32),
        compiler_params=_params(1),
        name="merge",
    )(h, y, zs, o, gate, *consts)


def _split_cols(t, sizes):
    out, start = [], 0
    for n in sizes:
        out.append(t[..., start:start + n])
        start += n
    return out


def _conv_shift_matrices():
    t = np.arange(SSD_STEP)
    return jnp.asarray(np.stack([(t[:, None] - t[None, :]) == (SSD_CONV - 1 - k) for k in range(SSD_CONV - 1)]),
                       dtype=BF16)


def _head_spread_matrix():
    src = np.arange(LANES)
    dst = np.arange(2 * SSD_D_INNER)
    same_head = (src[:, None] % SSD_HEADS) == (dst[None, :] % SSD_D_INNER) // SSD_HEAD_DIM
    same_half = ((src[:, None] // SSD_HEADS) % 2) == dst[None, :] // SSD_D_INNER
    return jnp.asarray(same_head & same_half, dtype=BF16)


def _layer_weights(l, p):
    half = MLA_ROPE // 2
    w_z, w_xbc, w_dt, w_cq, w_ckv, w_kr, w_gate = _split_cols(
        p["w_in"][l], (SSD_D_INNER, SSD_CONV_DIM, SSD_HEADS, MLA_Q_LORA, MLA_KV_LORA, MLA_ROPE, 2 * D_MODEL))
    w_krx = jnp.concatenate([w_kr, w_kr[:, half:], w_kr[:, :half]], axis=1)
    w_ukv = p["w_ukv"][l].reshape(MLA_KV_LORA, MLA_HEADS, MLA_NOPE + MLA_V)
    w_uk = w_ukv[:, :, :MLA_NOPE].reshape(MLA_KV_LORA, MLA_HEADS * MLA_NOPE)
    w_uv = w_ukv[:, :, MLA_NOPE:].reshape(MLA_KV_LORA, MLA_HEADS * MLA_V)
    g_q = p["q_norm"][l] * float(MLA_QK ** -0.5 * np.log2(np.e))
    g_k = p["k_norm"][l]
    g1, g2 = g_k[MLA_NOPE:MLA_NOPE + half], g_k[MLA_NOPE + half:]
    rope_inv = 1.0 / (ROPE_THETA ** (jnp.arange(0, MLA_ROPE, 2, dtype=F32) / MLA_ROPE))
    return {
        "g_ffn1": p["ln_ffn1"][l][None, :],
        "ffn1_w13": p["ffn1_w13"][l].astype(BF16),
        "ffn1_w2": p["ffn1_w2"][l].astype(BF16),
        "g_ffn2": p["ln_ffn2"][l][None, :],
        "ffn2_w13": p["ffn2_w13"][l].astype(BF16),
        "ffn2_w2": p["ffn2_w2"][l].astype(BF16),
        "g_mix": p["ln_mix"][l][None, :],
        "w_z": w_z.astype(BF16),
        "w_xbc": w_xbc.astype(BF16),
        "w_dt4": jnp.tile(w_dt, (1, LANES // SSD_HEADS)).astype(BF16),
        "w_dtT": w_dt.T.astype(BF16),
        "w_cq": w_cq.astype(BF16),
        "w_ckv": w_ckv.astype(BF16),
        "w_krx": w_krx.astype(BF16),
        "w_gate": w_gate.astype(BF16),
        "conv_w": p["conv_w"][l],
        "conv_b": p["conv_b"][l][None, :],
        "dt_bias4": jnp.tile(p["dt_bias"][l], LANES // SSD_HEADS)[None, :],
        "dt_bias_col": p["dt_bias"][l][:, None],
        "a_log4": jnp.tile(p["a_log"][l], LANES // SSD_HEADS)[None, :],
        "head_spread": _head_spread_matrix(),
        "conv_shift": _conv_shift_matrices(),
        "a_log_col": p["a_log"][l][:, None],
        "d_skip_x": jnp.repeat(p["d_skip"][l], SSD_HEAD_DIM)[None, :],
        "g_ssd": p["ssd_norm"][l][None, :],
        "w_ssd_out": p["w_ssd_out"][l].astype(BF16),
        "rope_inv": rope_inv[:, None],
        "g_q_lora": p["q_lora_norm"][l][None, :],
        "w_uqT": p["w_uq"][l].T.astype(BF16),
        "g_kv_lora": p["kv_lora_norm"][l][None, :],
        "w_uk": w_uk.astype(BF16),
        "w_uvT": w_uv.T.astype(BF16),
        "g_q_col": g_q[:, None],
        "g_k_nope": g_k[None, :MLA_NOPE],
        "g_k_rope_col": jnp.concatenate([g1, g2, -g2, g1])[:, None],
        "w_mla_out": p["w_mla_out"][l].astype(BF16),
        "w_o": p["w_o"][l].astype(BF16),
    }


def kernel(x, positions, ln_ffn1, ffn1_w13, ffn1_w2, ln_mix, w_in, conv_w, conv_b, dt_bias, a_log, d_skip,
           ssd_norm, w_ssd_out, q_lora_norm, w_uq, kv_lora_norm, w_ukv, q_norm, k_norm, w_mla_out, w_o,
           ln_ffn2, ffn2_w13, ffn2_w2):
    bsz, seq, d_model = x.shape
    assert d_model == D_MODEL and seq % FLASH_Q_BLOCK == 0 and seq % SSD_STEP == 0
    assert seq % TOKEN_TILE == 0 and seq % IN_PROJ_TILE == 0
    params = dict(ln_ffn1=ln_ffn1, ffn1_w13=ffn1_w13, ffn1_w2=ffn1_w2, ln_mix=ln_mix, w_in=w_in, conv_w=conv_w,
                  conv_b=conv_b, dt_bias=dt_bias, a_log=a_log, d_skip=d_skip, ssd_norm=ssd_norm,
                  w_ssd_out=w_ssd_out, q_lora_norm=q_lora_norm, w_uq=w_uq, kv_lora_norm=kv_lora_norm, w_ukv=w_ukv,
                  q_norm=q_norm, k_norm=k_norm, w_mla_out=w_mla_out, w_o=w_o, ln_ffn2=ln_ffn2, ffn2_w13=ffn2_w13,
                  ffn2_w2=ffn2_w2)
    tokens = bsz * seq
    pos_row = positions.reshape(bsz, 1, seq)
    h = x.reshape(tokens, D_MODEL)
    for l in range(ln_ffn1.shape[0]):
        w = _layer_weights(l, params)
        h = _ffn(h, w["g_ffn1"], w["ffn1_w13"], w["ffn1_w2"])
        zs, xbc, dt, dtT, cq, ckv, krx, gate = _in_proj(h, w["g_mix"], w, bsz, seq)
        y = _ssd(xbc, dt, dtT, w, bsz, seq)
        qT, k, vT = _mla_prep(cq, ckv, krx, pos_row, w, bsz, seq)
        o = _flash(qT, k, vT, bsz, seq)
        h = _merge(h, y.reshape(tokens, SSD_D_INNER), zs.reshape(tokens, SSD_D_INNER),
                   o.reshape(tokens, MLA_HEADS * MLA_V), gate.reshape(tokens, 2 * D_MODEL), w)
        h = _ffn(h, w["g_ffn2"], w["ffn2_w13"], w["ffn2_w2"])
    return h.reshape(bsz, seq, D_MODEL)
```

```python
import jax
import jax.numpy as jnp
import numpy as np
from jax import lax
from jax.experimental import pallas as pl
from jax.experimental.pallas import tpu as pltpu

F32 = jnp.float32
BF16 = jnp.bfloat16

D_MODEL = 1024
D_FF = 2816
SSD_D_INNER = 2048
SSD_HEAD_DIM = 64
SSD_HEADS = 32
SSD_GROUPS = 4
SSD_HPG = 8
SSD_STATE = 128
SSD_CONV = 4
SSD_CHUNK = 128
SSD_BC = SSD_GROUPS * SSD_STATE
SSD_CONV_DIM = SSD_D_INNER + 2 * SSD_BC
MLA_HEADS = 8
MLA_Q_LORA = 512
MLA_KV_LORA = 256
MLA_NOPE = 128
MLA_ROPE = 64
MLA_V = 128
MLA_QK = MLA_NOPE + MLA_ROPE
MLA_QK_PAD = 256
ROPE_THETA = 10000.0
EPS = 1e-6

LANES = 128
VMEM_LIMIT_BYTES = 56 * 1024 * 1024

TOKEN_TILE = 512
IN_PROJ_TILE = 256
SSD_STEP = 512
ATTN_BLOCK = 512
FLASH_Q_BLOCK = 1024
CONV_HALO = 8

NT_DIMS = (((1,), (1,)), ((), ()))
TN_DIMS = (((0,), (0,)), ((), ()))


def _params(n_grid_axes):
    return pltpu.CompilerParams(
        dimension_semantics=("arbitrary",) * n_grid_axes,
        vmem_limit_bytes=VMEM_LIMIT_BYTES,
    )


def _resident(x, layer=None):
    if layer is None:
        zeros = (0,) * x.ndim
        return pl.BlockSpec(x.shape, lambda *_: zeros, pipeline_mode=pl.Buffered(1))
    index = (layer,) + (0,) * (x.ndim - 1)
    return pl.BlockSpec((None,) + x.shape[1:], lambda *_: index, pipeline_mode=pl.Buffered(1))


def _rms(x, gain):
    return x * lax.rsqrt(jnp.mean(x * x, axis=-1, keepdims=True) + EPS) * gain


def _silu(x):
    return x * jax.nn.sigmoid(x)


def _ffn_kernel(h_ref, g_ref, w13_ref, w2_ref, o_ref):
    x = h_ref[...]
    xb = _rms(x, g_ref[...]).astype(BF16)
    gate = jnp.dot(xb, w13_ref[:, :D_FF], preferred_element_type=F32)
    up = jnp.dot(xb, w13_ref[:, D_FF:], preferred_element_type=F32)
    hidden = (_silu(gate) * up).astype(BF16)
    o_ref[...] = x + 0.5 * jnp.dot(hidden, w2_ref[...], preferred_element_type=F32)


def _ffn(h, gain, w13, w2, layer):
    t = h.shape[0]
    return pl.pallas_call(
        _ffn_kernel,
        grid=(t // TOKEN_TILE,),
        in_specs=[
            pl.BlockSpec((TOKEN_TILE, D_MODEL), lambda i: (i, 0)),
            _resident(gain, layer),
            _resident(w13, layer),
            _resident(w2, layer),
        ],
        out_specs=pl.BlockSpec((TOKEN_TILE, D_MODEL), lambda i: (i, 0)),
        out_shape=jax.ShapeDtypeStruct((t, D_MODEL), F32),
        compiler_params=_params(1),
        name="ffn",
    )(h, gain, w13, w2)


def _in_proj_kernel(h_ref, g_ref, wz_ref, wxbc_ref, wdtkr_ref, wdtT_ref, wcq_ref, wckv_ref, wg_ref,
                    zs_ref, xbc_ref, dt_ref, dtT_ref, cq_ref, ckv_ref, krx_ref, gate_ref):
    xb = _rms(h_ref[...], g_ref[...]).astype(BF16)

    def proj(w_ref):
        return jnp.dot(xb, w_ref[...], preferred_element_type=F32)

    zs_ref[...] = _silu(proj(wz_ref)).astype(BF16)
    gate_ref[...] = jax.nn.sigmoid(proj(wg_ref)).astype(BF16)

    xbc_ref[...] = proj(wxbc_ref).astype(BF16)
    dt_krx = proj(wdtkr_ref)
    dt_ref[...] = dt_krx[:, :LANES]
    krx_ref[...] = dt_krx[:, LANES:]
    dtT_ref[...] = lax.dot_general(wdtT_ref[...], xb, NT_DIMS, preferred_element_type=F32)
    cq_ref[...] = proj(wcq_ref).astype(BF16)
    ckv_ref[...] = proj(wckv_ref).astype(BF16)


def _in_proj(h, w, layer, bsz, seq):
    tm = IN_PROJ_TILE
    nt = seq // tm
    widths = (SSD_D_INNER, SSD_CONV_DIM, LANES, None, MLA_Q_LORA, MLA_KV_LORA, LANES, 2 * D_MODEL)
    dtypes = (BF16, BF16, F32, F32, BF16, BF16, F32, BF16)
    out_specs, out_shape = [], []
    for width, dtype in zip(widths, dtypes):
        if width is None:
            out_specs.append(pl.BlockSpec((None, SSD_HEADS, tm), lambda b, i: (b, 0, i)))
            out_shape.append(jax.ShapeDtypeStruct((bsz, SSD_HEADS, seq), dtype))
        else:
            out_specs.append(pl.BlockSpec((None, tm, width), lambda b, i: (b, i, 0)))
            out_shape.append(jax.ShapeDtypeStruct((bsz, seq, width), dtype))
    weights = (w["w_z"], w["w_xbc"], w["w_dt4_krx"], w["w_dtT"], w["w_cq"], w["w_ckv"], w["w_gate"])
    return pl.pallas_call(
        _in_proj_kernel,
        grid=(bsz, nt),
        in_specs=[pl.BlockSpec((None, tm, D_MODEL), lambda b, i: (b, i, 0)), _resident(w["g_mix"], layer)]
        + [_resident(x, layer) for x in weights],
        out_specs=out_specs,
        out_shape=out_shape,
        compiler_params=_params(2),
        name="in_proj",
    )(h.reshape(bsz, seq, D_MODEL), w["g_mix"], *weights)


def _mla_prep_kernel(cq_ref, ckv_ref, krx_ref, pos_ref, inv_ref, gql_ref, wuqT_ref, gkvl_ref, wuk_ref, wuvT_ref,
                     gq_ref, gkn_ref, gkr_ref, qT_ref, k_ref, vT_ref):
    tm = cq_ref.shape[0]
    half = MLA_ROPE // 2
    ang = inv_ref[...] * pos_ref[...].astype(F32)
    cos, sin = jnp.cos(ang), jnp.sin(ang)

    qn = _rms(cq_ref[...].astype(F32), gql_ref[...]).astype(BF16)
    qT = lax.dot_general(wuqT_ref[...], qn, NT_DIMS, preferred_element_type=F32)
    gq = jnp.broadcast_to(gq_ref[...], (MLA_QK, tm))
    for h in range(MLA_HEADS):
        blk = qT[h * MLA_QK:(h + 1) * MLA_QK]
        inv_rms = lax.rsqrt(jnp.sum(blk * blk, axis=0, keepdims=True) * (1.0 / MLA_QK) + EPS)
        xn = blk * inv_rms * gq
        t1 = xn[MLA_NOPE:MLA_NOPE + half]
        t2 = xn[MLA_NOPE + half:]
        qT_ref[h, 0:MLA_NOPE, :] = xn[:MLA_NOPE].astype(BF16)
        qT_ref[h, MLA_NOPE:MLA_NOPE + half, :] = (t1 * cos - t2 * sin).astype(BF16)
        qT_ref[h, MLA_NOPE + half:MLA_QK, :] = (t2 * cos + t1 * sin).astype(BF16)
        qT_ref[h, MLA_QK:, :] = jnp.zeros((MLA_QK_PAD - MLA_QK, tm), BF16)

    kvn = _rms(ckv_ref[...].astype(F32), gkvl_ref[...]).astype(BF16)
    kn = jnp.dot(kvn, wuk_ref[...], preferred_element_type=F32)
    vT = lax.dot_general(wuvT_ref[...], kvn, NT_DIMS, preferred_element_type=F32)
    vT_ref[...] = vT.reshape(MLA_HEADS, MLA_V, tm).astype(BF16)

    krx = krx_ref[...]
    rope_ss = 0.5 * jnp.sum(krx * krx, axis=-1, keepdims=True)
    table = (jnp.concatenate([cos, cos, sin, sin], axis=0) * gkr_ref[...]).T
    prod = krx * table
    k_rope = prod + pltpu.roll(prod, MLA_ROPE, 1)
    gkn = gkn_ref[...]
    for h in range(MLA_HEADS):
        knh = kn[:, h * MLA_NOPE:(h + 1) * MLA_NOPE]
        ss = jnp.sum(knh * knh, axis=-1, keepdims=True) + rope_ss
        inv_rms = lax.rsqrt(ss * (1.0 / MLA_QK) + EPS)
        k_ref[h, :, 0:MLA_NOPE] = (knh * inv_rms * gkn).astype(BF16)
        k_ref[h, :, MLA_NOPE:] = (k_rope * inv_rms).astype(BF16)


def _mla_prep(cq, ckv, krx, pos_row, w, layer, bsz, seq):
    tm = ATTN_BLOCK
    nt = seq // tm
    per_q = FLASH_Q_BLOCK // tm
    consts = (w["g_q_lora"], w["w_uqT"], w["g_kv_lora"], w["w_uk"], w["w_uvT"],
              w["g_q_col"], w["g_k_nope"], w["g_k_rope_col"])
    in_specs = [
        pl.BlockSpec((None, tm, MLA_Q_LORA), lambda b, i: (b, i, 0)),
        pl.BlockSpec((None, tm, MLA_KV_LORA), lambda b, i: (b, i, 0)),
        pl.BlockSpec((None, tm, LANES), lambda b, i: (b, i, 0)),
        pl.BlockSpec((None, 1, tm), lambda b, i: (b, 0, i)),
        _resident(w["rope_inv"]),
    ] + [_resident(x, layer) for x in consts]
    out_specs = [
        pl.BlockSpec((None, MLA_HEADS, None, MLA_QK_PAD, tm), lambda b, i: (b, 0, i // per_q, 0, i % per_q)),
        pl.BlockSpec((None, MLA_HEADS, tm, MLA_QK_PAD), lambda b, i: (b, 0, i, 0)),
        pl.BlockSpec((None, MLA_HEADS, None, MLA_V, tm), lambda b, i: (b, 0, i, 0, 0)),
    ]
    out_shape = [
        jax.ShapeDtypeStruct((bsz, MLA_HEADS, seq // FLASH_Q_BLOCK, MLA_QK_PAD, FLASH_Q_BLOCK), BF16),
        jax.ShapeDtypeStruct((bsz, MLA_HEADS, seq, MLA_QK_PAD), BF16),
        jax.ShapeDtypeStruct((bsz, MLA_HEADS, nt, MLA_V, tm), BF16),
    ]
    return pl.pallas_call(
        _mla_prep_kernel,
        grid=(bsz, nt),
        in_specs=in_specs,
        out_specs=out_specs,
        out_shape=out_shape,
        compiler_params=_params(2),
        name="mla_prep",
    )(cq, ckv, krx, pos_row, w["rope_inv"], *consts)


def _flash_kernel(qT_ref, k_ref, vT_ref, o_ref, s_ref, cmax_ref, m_ref, l_ref, acc_ref):
    bq, bk = FLASH_Q_BLOCK, ATTN_BLOCK
    kpq = bq // bk
    nq = qT_ref.shape[0]

    def scores(qi, j, slot):
        k_blk = k_ref[pl.ds(pl.multiple_of(j * bk, bk), bk), :]
        sT = jnp.dot(k_blk, qT_ref[qi], preferred_element_type=F32)
        s_ref[slot] = sT
        cmax_ref[slot] = jnp.max(sT, axis=0, keepdims=True)

    def softmax_pv(j, sT, cmax, qs):
        m_prev = m_ref[:, qs]
        m_new = jnp.maximum(m_prev, cmax)
        alpha = jnp.exp2(m_prev - m_new)
        p = jnp.exp2(sT - m_new)
        l_ref[:, qs] = alpha * l_ref[:, qs] + jnp.sum(p, axis=0, keepdims=True)
        pv = jnp.dot(vT_ref[j], p.astype(BF16), preferred_element_type=F32)
        acc_ref[:, qs] = alpha * acc_ref[:, qs] + pv
        m_ref[:, qs] = m_new

    def consume(j, slot):
        softmax_pv(j, s_ref[slot], cmax_ref[slot], slice(None))

    def query_block(i, carry):
        m_ref[...] = jnp.full(m_ref.shape, -jnp.inf, F32)
        l_ref[...] = jnp.zeros(l_ref.shape, F32)
        acc_ref[...] = jnp.zeros(acc_ref.shape, F32)

        def two_blocks(j):
            scores(i, j + 1, 1)
            consume(j, 0)
            scores(i, j + 2, 0)
            consume(j + 1, 1)

        def four_blocks(t, c):
            two_blocks(4 * t)
            two_blocks(4 * t + 2)
            return c

        fours = lax.shift_right_logical(i, 1)
        lax.fori_loop(0, fours, four_blocks, 0)

        @pl.when((i & 1) == 1)
        def _():
            two_blocks(4 * fours)

        d0 = i * kpq
        upper = slice(bk, bq)
        k_hi = k_ref[pl.ds(pl.multiple_of((d0 + 1) * bk, bk), bk), :]
        s_hi = jnp.dot(k_hi, qT_ref[i, :, upper], preferred_element_type=F32)
        kpos = lax.broadcasted_iota(jnp.int32, (bk, bk), 0)
        qpos = lax.broadcasted_iota(jnp.int32, (bk, bk), 1)
        tri = kpos <= qpos
        s_lo = s_ref[0]
        s_lo = jnp.concatenate([jnp.where(tri, s_lo[:, :bk], -jnp.inf), s_lo[:, upper]], axis=1)
        softmax_pv(d0, s_lo, jnp.max(s_lo, axis=0, keepdims=True), slice(None))
        scores(jnp.minimum(i + 1, nq - 1), 0, 0)
        s_hi = jnp.where(tri, s_hi, -jnp.inf)
        softmax_pv(d0 + 1, s_hi, jnp.max(s_hi, axis=0, keepdims=True), upper)
        o_ref[pl.ds(pl.multiple_of(i * bq, bq), bq), :] = (acc_ref[...] / l_ref[...]).T.astype(BF16)
        return carry

    scores(0, 0, 0)
    lax.fori_loop(0, nq, query_block, 0)


def _flash(qT, k, vT, bsz, seq):
    bq, bk = FLASH_Q_BLOCK, ATTN_BLOCK
    assert bq == 2 * bk and seq % bq == 0
    nq = seq // bq
    return pl.pallas_call(
        _flash_kernel,
        grid=(bsz, MLA_HEADS),
        in_specs=[
            pl.BlockSpec((None, None, nq, MLA_QK_PAD, bq), lambda b, h: (b, h, 0, 0, 0)),
            pl.BlockSpec((None, None, seq, MLA_QK_PAD), lambda b, h: (b, h, 0, 0)),
            pl.BlockSpec((None, None, seq // bk, MLA_V, bk), lambda b, h: (b, h, 0, 0, 0)),
        ],
        out_specs=pl.BlockSpec((None, seq, MLA_V), lambda b, h: (b, 0, h)),
        out_shape=jax.ShapeDtypeStruct((bsz, seq, MLA_HEADS * MLA_V), BF16),
        scratch_shapes=[
            pltpu.VMEM((2, bk, bq), F32),
            pltpu.VMEM((2, 1, bq), F32),
            pltpu.VMEM((1, bq), F32),
            pltpu.VMEM((1, bq), F32),
            pltpu.VMEM((MLA_V, bq), F32),
        ],
        compiler_params=_params(2),
        name="flash",
    )(qT, k, vT)


def _softplus(x):
    return jnp.maximum(x, 0.0) + jnp.log1p(jnp.exp(-jnp.abs(x)))


def _ssd_kernel(xbc_ref, dt_ref, dtT_ref, shift_ref, e4_ref, convw_ref, convb_ref, dtb_ref, dtbT_ref, alog_ref,
                alogT_ref, dskip_ref, y_ref, state_ref, halo_ref):
    step_len = xbc_ref.shape[0]
    cl = SSD_CHUNK

    @pl.when(pl.program_id(1) == 0)
    def _():
        state_ref[...] = jnp.zeros(state_ref.shape, F32)
        halo_ref[...] = jnp.zeros(halo_ref.shape, F32)

    def conv_silu(r0, tail):
        taps = SSD_CONV - 1
        xh = xbc_ref[r0:r0 + cl, :]
        xf = xh.astype(F32)
        delayed = jnp.dot(shift_ref[...], xh, preferred_element_type=F32)
        delayed = delayed.reshape(cl // CONV_HALO, taps, CONV_HALO, SSD_CONV_DIM)
        conv = convb_ref[...] + convw_ref[taps:SSD_CONV, :] * xf
        for k in range(taps):
            conv = conv + convw_ref[k:k + 1, :] * delayed[:, k].reshape(cl, SSD_CONV_DIM)
        head_in = jnp.concatenate([tail, xf[:CONV_HALO]], axis=0)
        head = convb_ref[...] + convw_ref[taps:SSD_CONV, :] * xf[:CONV_HALO]
        for k in range(taps):
            start = CONV_HALO - taps + k
            head = head + convw_ref[k:k + 1, :] * head_in[start:start + CONV_HALO]
        return _silu(jnp.concatenate([head, conv[CONV_HALO:]], axis=0)), xf[cl - CONV_HALO:]

    tail = halo_ref[...]
    xcs = []
    for c in range(step_len // cl):
        xc_c, tail = conv_silu(c * cl, tail)
        xcs.append(xc_c)
    halo_ref[...] = tail

    dt_all = _softplus(dt_ref[...] + dtb_ref[...])
    dtT_all = _softplus(dtT_ref[...] + dtbT_ref[...])
    a_row = -jnp.exp(alog_ref[...])
    a_col = -jnp.exp(alogT_ref[...])

    row = lax.broadcasted_iota(jnp.int32, (cl, cl), 0)
    col = lax.broadcasted_iota(jnp.int32, (cl, cl), 1)
    causal = col <= row
    tril = causal.astype(F32)
    triu = (row <= col).astype(F32)
    lane = lax.broadcasted_iota(jnp.int32, (cl, LANES), 1)
    left_head = lane < SSD_HEAD_DIM
    odd_copy = (lane & SSD_HEADS) != 0

    for c in range(step_len // cl):
        r0 = c * cl
        xc = xcs[c]
        dt = dt_all[r0:r0 + cl]
        dtT = dtT_all[:, r0:r0 + cl]
        acum = jnp.dot(tril, dt * a_row, preferred_element_type=F32, precision=lax.Precision.HIGHEST)
        acumT = jnp.dot(dtT * a_col, triu, preferred_element_type=F32, precision=lax.Precision.HIGHEST)
        acumT_dt = acumT - jnp.log(dtT)
        last = acum[cl - 1:cl, :]
        fac = jnp.where(odd_copy, dt * jnp.exp(last - acum), jnp.exp(acum))
        hi = fac.astype(BF16).astype(F32)
        split = jnp.where(lane < 2 * SSD_HEADS, hi, fac - hi).astype(BF16)

        for g in range(SSD_GROUPS):
            gw = SSD_HPG * SSD_HEAD_DIM
            gsl = slice(g * gw, (g + 1) * gw)
            bg = xc[:, SSD_D_INNER + g * SSD_STATE:SSD_D_INNER + (g + 1) * SSD_STATE].astype(BF16)
            cg = xc[:, SSD_D_INNER + SSD_BC + g * SSD_STATE:
                    SSD_D_INNER + SSD_BC + (g + 1) * SSD_STATE].astype(BF16)
            xs = xc[:, gsl]
            xs_b = xs.astype(BF16)
            spread = jnp.dot(split, e4_ref[g], preferred_element_type=F32)
            y_off_scale = spread[:, :gw]
            xw = (xs * spread[:, gw:]).astype(BF16)
            state_decay = y_off_scale[cl - 1:cl, :]
            cb = lax.dot_general(cg, bg, NT_DIMS, preferred_element_type=F32)
            st = state_ref[:, gsl]
            y_off = jnp.dot(cg, st.astype(BF16), preferred_element_type=F32) * y_off_scale
            for j in range(SSD_HPG // 2):
                mats = []
                for h in (g * SSD_HPG + 2 * j, g * SSD_HPG + 2 * j + 1):
                    seg = acum[:, h:h + 1] - acumT_dt[h:h + 1, :]
                    mats.append(cb * jnp.exp(jnp.where(causal, seg, -jnp.inf)))
                lhs = jnp.concatenate(mats, axis=1).astype(BF16)
                lsl = slice(2 * j * SSD_HEAD_DIM, (2 * j + 2) * SSD_HEAD_DIM)
                xp = xs_b[:, lsl]
                zero = jnp.zeros_like(xp)
                rhs = jnp.concatenate([jnp.where(left_head, xp, zero), jnp.where(left_head, zero, xp)],
                                      axis=0)
                y_diag = jnp.dot(lhs, rhs, preferred_element_type=F32)
                csl = slice(g * gw + 2 * j * SSD_HEAD_DIM, g * gw + (2 * j + 2) * SSD_HEAD_DIM)
                y = y_diag + y_off[:, lsl] + xs[:, lsl] * dskip_ref[:, csl]
                y_ref[r0:r0 + cl, csl] = y.astype(BF16)
            state_ref[:, gsl] = st * state_decay + lax.dot_general(bg, xw, TN_DIMS, preferred_element_type=F32)


def _ssd(xbc, dt, dtT, w, layer, bsz, seq):
    ls = SSD_STEP
    shared = (w["conv_shift"], w["head_spread"])
    consts = (w["conv_w"], w["conv_b"], w["dt_bias4"], w["dt_bias_col"], w["a_log4"], w["a_log_col"], w["d_skip_x"])
    return pl.pallas_call(
        _ssd_kernel,
        grid=(bsz, seq // ls),
        in_specs=[
            pl.BlockSpec((None, ls, SSD_CONV_DIM), lambda b, i: (b, i, 0)),
            pl.BlockSpec((None, ls, LANES), lambda b, i: (b, i, 0)),
            pl.BlockSpec((None, SSD_HEADS, ls), lambda b, i: (b, 0, i)),
        ] + [_resident(x) for x in shared] + [_resident(x, layer) for x in consts],
        out_specs=pl.BlockSpec((None, ls, SSD_D_INNER), lambda b, i: (b, i, 0)),
        out_shape=jax.ShapeDtypeStruct((bsz, seq, SSD_D_INNER), BF16),
        scratch_shapes=[pltpu.VMEM((SSD_STATE, SSD_D_INNER), F32), pltpu.VMEM((CONV_HALO, SSD_CONV_DIM), F32)],
        compiler_params=_params(2),
        name="ssd",
    )(xbc, dt, dtT, *shared, *consts)


def _merge_kernel(h_ref, y_ref, zs_ref, o_ref, gate_ref, gssd_ref, wssd_ref, wmla_ref, wo_ref, out_ref):
    yg = y_ref[...].astype(F32) * zs_ref[...].astype(F32)
    y_ssd = jnp.dot(_rms(yg, gssd_ref[...]).astype(BF16), wssd_ref[...], preferred_element_type=F32)
    y_mla = jnp.dot(o_ref[...], wmla_ref[...], preferred_element_type=F32)
    gate = gate_ref[...].astype(F32)
    merged = gate[:, :D_MODEL] * y_ssd + gate[:, D_MODEL:] * y_mla
    out_ref[...] = h_ref[...] + jnp.dot(merged.astype(BF16), wo_ref[...], preferred_element_type=F32)


def _merge(h, y, zs, o, gate, w, layer):
    t = h.shape[0]
    tm = TOKEN_TILE
    consts = (w["g_ssd"], w["w_ssd_out"], w["w_mla_out"], w["w_o"])

    def rows(width):
        return pl.BlockSpec((tm, width), lambda i: (i, 0))

    return pl.pallas_call(
        _merge_kernel,
        grid=(t // tm,),
        in_specs=[rows(D_MODEL), rows(SSD_D_INNER), rows(SSD_D_INNER), rows(MLA_HEADS * MLA_V), rows(2 * D_MODEL)]
        + [_resident(x, layer) for x in consts],
        out_specs=rows(D_MODEL),
        out_shape=jax.ShapeDtypeStruct((t, D_MODEL), F32),
        compiler_params=_params(1),
        name="merge",
    )(h, y, zs, o, gate, *consts)


def _split_cols(t, sizes):
    out, start = [], 0
    for n in sizes:
        out.append(t[..., start:start + n])
        start += n
    return out


def _conv_shift_matrix():
    taps = SSD_CONV - 1
    out_row = np.arange(taps * SSD_CHUNK)
    group, rest = np.divmod(out_row, taps * CONV_HALO)
    k, r = np.divmod(rest, CONV_HALO)
    src = group * CONV_HALO + r - (taps - k)
    return jnp.asarray(src[:, None] == np.arange(SSD_CHUNK)[None, :], dtype=BF16)


def _head_spread_matrix():
    gw = SSD_HPG * SSD_HEAD_DIM
    src = np.arange(LANES)
    dst = np.arange(2 * gw)
    slabs = []
    for g in range(SSD_GROUPS):
        same_head = (src[:, None] % SSD_HEADS) == g * SSD_HPG + (dst[None, :] % gw) // SSD_HEAD_DIM
        same_half = ((src[:, None] // SSD_HEADS) % 2) == dst[None, :] // gw
        slabs.append(same_head & same_half)
    return jnp.asarray(np.stack(slabs), dtype=BF16)


def _prepare_weights(p):
    half = MLA_ROPE // 2
    depth = p["w_in"].shape[0]
    bf16 = lambda t: t.astype(BF16)
    row = lambda t: t[:, None, :]
    col = lambda t: t[:, :, None]
    transpose = lambda t: jnp.swapaxes(t, -1, -2)
    w_z, w_xbc, w_dt, w_cq, w_ckv, w_kr, w_gate = _split_cols(
        p["w_in"], (SSD_D_INNER, SSD_CONV_DIM, SSD_HEADS, MLA_Q_LORA, MLA_KV_LORA, MLA_ROPE, 2 * D_MODEL))
    w_krx = jnp.concatenate([w_kr, w_kr[..., half:], w_kr[..., :half]], axis=-1)
    w_ukv = p["w_ukv"].reshape(depth, MLA_KV_LORA, MLA_HEADS, MLA_NOPE + MLA_V)
    w_uk = w_ukv[..., :MLA_NOPE].reshape(depth, MLA_KV_LORA, MLA_HEADS * MLA_NOPE)
    w_uv = w_ukv[..., MLA_NOPE:].reshape(depth, MLA_KV_LORA, MLA_HEADS * MLA_V)
    g_q = p["q_norm"] * float(MLA_QK ** -0.5 * np.log2(np.e))
    g_k = p["k_norm"]
    g1, g2 = g_k[:, MLA_NOPE:MLA_NOPE + half], g_k[:, MLA_NOPE + half:]
    rope_inv = 1.0 / (ROPE_THETA ** (jnp.arange(0, MLA_ROPE, 2, dtype=F32) / MLA_ROPE))
    copies = LANES // SSD_HEADS
    return {
        "g_ffn1": row(p["ln_ffn1"]),
        "ffn1_w13": bf16(p["ffn1_w13"]),
        "ffn1_w2": bf16(p["ffn1_w2"]),
        "g_ffn2": row(p["ln_ffn2"]),
        "ffn2_w13": bf16(p["ffn2_w13"]),
        "ffn2_w2": bf16(p["ffn2_w2"]),
        "g_mix": row(p["ln_mix"]),
        "w_z": bf16(w_z),
        "w_xbc": bf16(w_xbc),
        "w_dt4_krx": bf16(jnp.concatenate([jnp.tile(w_dt, (1, 1, copies)), w_krx], axis=-1)),
        "w_dtT": bf16(transpose(w_dt)),
        "w_cq": bf16(w_cq),
        "w_ckv": bf16(w_ckv),
        "w_gate": bf16(w_gate),
        "conv_w": p["conv_w"],
        "conv_b": row(p["conv_b"]),
        "dt_bias4": row(jnp.tile(p["dt_bias"], (1, copies))),
        "dt_bias_col": col(p["dt_bias"]),
        "a_log4": row(jnp.tile(p["a_log"], (1, copies))),
        "a_log_col": col(p["a_log"]),
        "d_skip_x": row(jnp.repeat(p["d_skip"], SSD_HEAD_DIM, axis=-1)),
        "g_ssd": row(p["ssd_norm"]),
        "w_ssd_out": bf16(p["w_ssd_out"]),
        "g_q_lora": row(p["q_lora_norm"]),
        "w_uqT": bf16(transpose(p["w_uq"])),
        "g_kv_lora": row(p["kv_lora_norm"]),
        "w_uk": bf16(w_uk),
        "w_uvT": bf16(transpose(w_uv)),
        "g_q_col": col(g_q),
        "g_k_nope": row(g_k[:, :MLA_NOPE]),
        "g_k_rope_col": col(jnp.concatenate([g1, g2, -g2, g1], axis=-1)),
        "w_mla_out": bf16(p["w_mla_out"]),
        "w_o": bf16(p["w_o"]),
        "head_spread": _head_spread_matrix(),
        "conv_shift": _conv_shift_matrix(),
        "rope_inv": rope_inv[:, None],
    }


def kernel(x, positions, ln_ffn1, ffn1_w13, ffn1_w2, ln_mix, w_in, conv_w, conv_b, dt_bias, a_log, d_skip,
           ssd_norm, w_ssd_out, q_lora_norm, w_uq, kv_lora_norm, w_ukv, q_norm, k_norm, w_mla_out, w_o,
           ln_ffn2, ffn2_w13, ffn2_w2):
    bsz, seq, d_model = x.shape
    assert d_model == D_MODEL and seq % FLASH_Q_BLOCK == 0 and seq % SSD_STEP == 0
    assert seq % TOKEN_TILE == 0 and seq % IN_PROJ_TILE == 0
    params = dict(ln_ffn1=ln_ffn1, ffn1_w13=ffn1_w13, ffn1_w2=ffn1_w2, ln_mix=ln_mix, w_in=w_in, conv_w=conv_w,
                  conv_b=conv_b, dt_bias=dt_bias, a_log=a_log, d_skip=d_skip, ssd_norm=ssd_norm,
                  w_ssd_out=w_ssd_out, q_lora_norm=q_lora_norm, w_uq=w_uq, kv_lora_norm=kv_lora_norm, w_ukv=w_ukv,
                  q_norm=q_norm, k_norm=k_norm, w_mla_out=w_mla_out, w_o=w_o, ln_ffn2=ln_ffn2, ffn2_w13=ffn2_w13,
                  ffn2_w2=ffn2_w2)
    tokens = bsz * seq
    pos_row = positions.reshape(bsz, 1, seq)
    h = x.reshape(tokens, D_MODEL)
    w = _prepare_weights(params)
    for layer in range(ln_ffn1.shape[0]):
        h = _ffn(h, w["g_ffn1"], w["ffn1_w13"], w["ffn1_w2"], layer)
        zs, xbc, dt, dtT, cq, ckv, krx, gate = _in_proj(h, w, layer, bsz, seq)
        y = _ssd(xbc, dt, dtT, w, layer, bsz, seq)
        qT, k, vT = _mla_prep(cq, ckv, krx, pos_row, w, layer, bsz, seq)
        o = _flash(qT, k, vT, bsz, seq)
        h = _merge(h, y.reshape(tokens, SSD_D_INNER), zs.reshape(tokens, SSD_D_INNER),
                   o.reshape(tokens, MLA_HEADS * MLA_V), gate.reshape(tokens, 2 * D_MODEL), w, layer)
        h = _ffn(h, w["g_ffn2"], w["ffn2_w13"], w["ffn2_w2"], layer)
    return h.reshape(bsz, seq, D_MODEL)
```

```python
import jax
import jax.numpy as jnp
import numpy as np
from jax import lax
from jax.experimental import pallas as pl
from jax.experimental.pallas import tpu as pltpu

F32 = jnp.float32
BF16 = jnp.bfloat16

D_MODEL = 1024
D_FF = 2816
SSD_D_INNER = 2048
SSD_HEAD_DIM = 64
SSD_HEADS = 32
SSD_GROUPS = 4
SSD_HPG = 8
SSD_STATE = 128
SSD_CONV = 4
SSD_CHUNK = 128
SSD_BC = SSD_GROUPS * SSD_STATE
SSD_CONV_DIM = SSD_D_INNER + 2 * SSD_BC
MLA_HEADS = 8
MLA_Q_LORA = 512
MLA_KV_LORA = 256
MLA_NOPE = 128
MLA_ROPE = 64
MLA_V = 128
MLA_QK = MLA_NOPE + MLA_ROPE
MLA_V_AUG = MLA_V + 16
MLA_QK_PAD = 256
ROPE_THETA = 10000.0
EPS = 1e-6

LANES = 128
VMEM_LIMIT_BYTES = 56 * 1024 * 1024

TOKEN_TILE = 512
IN_PROJ_TILE = 256
SSD_STEP = 512
ATTN_BLOCK = 512
FLASH_Q_BLOCK = 1024
CONV_HALO = 8

NT_DIMS = (((1,), (1,)), ((), ()))
TN_DIMS = (((0,), (0,)), ((), ()))


def _params(n_grid_axes):
    return pltpu.CompilerParams(
        dimension_semantics=("arbitrary",) * n_grid_axes,
        vmem_limit_bytes=VMEM_LIMIT_BYTES,
    )


def _resident(x, layer=None):
    if layer is None:
        zeros = (0,) * x.ndim
        return pl.BlockSpec(x.shape, lambda *_: zeros, pipeline_mode=pl.Buffered(1))
    index = (layer,) + (0,) * (x.ndim - 1)
    return pl.BlockSpec((None,) + x.shape[1:], lambda *_: index, pipeline_mode=pl.Buffered(1))


def _rms(x, gain):
    return x * lax.rsqrt(jnp.mean(x * x, axis=-1, keepdims=True) + EPS) * gain


def _silu(x):
    return x * jax.nn.sigmoid(x)


def _ffn_kernel(h_ref, g_ref, w13_ref, w2_ref, o_ref):
    x = h_ref[...]
    xb = _rms(x, g_ref[...]).astype(BF16)
    gate = jnp.dot(xb, w13_ref[:, :D_FF], preferred_element_type=F32)
    up = jnp.dot(xb, w13_ref[:, D_FF:], preferred_element_type=F32)
    hidden = (_silu(gate) * up).astype(BF16)
    o_ref[...] = x + 0.5 * jnp.dot(hidden, w2_ref[...], preferred_element_type=F32)


def _ffn(h, gain, w13, w2, layer):
    t = h.shape[0]
    return pl.pallas_call(
        _ffn_kernel,
        grid=(t // TOKEN_TILE,),
        in_specs=[
            pl.BlockSpec((TOKEN_TILE, D_MODEL), lambda i: (i, 0)),
            _resident(gain, layer),
            _resident(w13, layer),
            _resident(w2, layer),
        ],
        out_specs=pl.BlockSpec((TOKEN_TILE, D_MODEL), lambda i: (i, 0)),
        out_shape=jax.ShapeDtypeStruct((t, D_MODEL), F32),
        compiler_params=_params(1),
        name="ffn",
    )(h, gain, w13, w2)


def _in_proj_kernel(h_ref, g_ref, wz_ref, wxbc_ref, wdtkr_ref, wdtT_ref, wcq_ref, wckv_ref, wg_ref,
                    zs_ref, xbc_ref, dt_ref, dtT_ref, cq_ref, ckv_ref, krx_ref, gate_ref):
    xb = _rms(h_ref[...], g_ref[...]).astype(BF16)

    def proj(w_ref):
        return jnp.dot(xb, w_ref[...], preferred_element_type=F32)

    zs_ref[...] = _silu(proj(wz_ref)).astype(BF16)
    gate_ref[...] = jax.nn.sigmoid(proj(wg_ref)).astype(BF16)

    xbc_ref[...] = proj(wxbc_ref).astype(BF16)
    dt_krx = proj(wdtkr_ref)
    dt_ref[...] = dt_krx[:, :LANES]
    krx_ref[...] = dt_krx[:, LANES:]
    dtT_ref[...] = lax.dot_general(wdtT_ref[...], xb, NT_DIMS, preferred_element_type=F32)
    cq_ref[...] = proj(wcq_ref).astype(BF16)
    ckv_ref[...] = proj(wckv_ref).astype(BF16)


def _in_proj(h, w, layer, bsz, seq):
    tm = IN_PROJ_TILE
    nt = seq // tm
    widths = (SSD_D_INNER, SSD_CONV_DIM, LANES, None, MLA_Q_LORA, MLA_KV_LORA, LANES, 2 * D_MODEL)
    dtypes = (BF16, BF16, F32, F32, BF16, BF16, F32, BF16)
    out_specs, out_shape = [], []
    for width, dtype in zip(widths, dtypes):
        if width is None:
            out_specs.append(pl.BlockSpec((None, SSD_HEADS, tm), lambda b, i: (b, 0, i)))
            out_shape.append(jax.ShapeDtypeStruct((bsz, SSD_HEADS, seq), dtype))
        else:
            out_specs.append(pl.BlockSpec((None, tm, width), lambda b, i: (b, i, 0)))
            out_shape.append(jax.ShapeDtypeStruct((bsz, seq, width), dtype))
    weights = (w["w_z"], w["w_xbc"], w["w_dt4_krx"], w["w_dtT"], w["w_cq"], w["w_ckv"], w["w_gate"])
    return pl.pallas_call(
        _in_proj_kernel,
        grid=(bsz, nt),
        in_specs=[pl.BlockSpec((None, tm, D_MODEL), lambda b, i: (b, i, 0)), _resident(w["g_mix"], layer)]
        + [_resident(x, layer) for x in weights],
        out_specs=out_specs,
        out_shape=out_shape,
        compiler_params=_params(2),
        name="in_proj",
    )(h.reshape(bsz, seq, D_MODEL), w["g_mix"], *weights)


def _mla_prep_kernel(cq_ref, ckv_ref, krx_ref, pos_ref, inv_ref, gql_ref, wuqT_ref, gkvl_ref, wuk_ref, wuvT_ref,
                     gq_ref, gkn_ref, gkr_ref, qT_ref, k_ref, vT_ref):
    tm = cq_ref.shape[0]
    half = MLA_ROPE // 2
    ang = inv_ref[...] * pos_ref[...].astype(F32)
    cos, sin = jnp.cos(ang), jnp.sin(ang)

    qn = _rms(cq_ref[...].astype(F32), gql_ref[...]).astype(BF16)
    qT = lax.dot_general(wuqT_ref[...], qn, NT_DIMS, preferred_element_type=F32)
    gq = jnp.broadcast_to(gq_ref[...], (MLA_QK, tm))
    for h in range(MLA_HEADS):
        blk = qT[h * MLA_QK:(h + 1) * MLA_QK]
        inv_rms = lax.rsqrt(jnp.sum(blk * blk, axis=0, keepdims=True) * (1.0 / MLA_QK) + EPS)
        xn = blk * inv_rms * gq
        t1 = xn[MLA_NOPE:MLA_NOPE + half]
        t2 = xn[MLA_NOPE + half:]
        qT_ref[h, 0:MLA_NOPE, :] = xn[:MLA_NOPE].astype(BF16)
        qT_ref[h, MLA_NOPE:MLA_NOPE + half, :] = (t1 * cos - t2 * sin).astype(BF16)
        qT_ref[h, MLA_NOPE + half:MLA_QK, :] = (t2 * cos + t1 * sin).astype(BF16)
        qT_ref[h, MLA_QK:, :] = jnp.zeros((MLA_QK_PAD - MLA_QK, tm), BF16)

    kvn = _rms(ckv_ref[...].astype(F32), gkvl_ref[...]).astype(BF16)
    kn = jnp.dot(kvn, wuk_ref[...], preferred_element_type=F32)
    vT = lax.dot_general(wuvT_ref[...], kvn, NT_DIMS, preferred_element_type=F32)
    vT_ref[:, 0:MLA_V, :] = vT.reshape(MLA_HEADS, MLA_V, tm).astype(BF16)
    ones_row = (lax.broadcasted_iota(jnp.int32, (MLA_HEADS, MLA_V_AUG - MLA_V, tm), 1) == 0).astype(BF16)
    vT_ref[:, MLA_V:, :] = ones_row

    krx = krx_ref[...]
    rope_ss = 0.5 * jnp.sum(krx * krx, axis=-1, keepdims=True)
    table = (jnp.concatenate([cos, cos, sin, sin], axis=0) * gkr_ref[...]).T
    prod = krx * table
    k_rope = prod + pltpu.roll(prod, MLA_ROPE, 1)
    gkn = gkn_ref[...]
    for h in range(MLA_HEADS):
        knh = kn[:, h * MLA_NOPE:(h + 1) * MLA_NOPE]
        ss = jnp.sum(knh * knh, axis=-1, keepdims=True) + rope_ss
        inv_rms = lax.rsqrt(ss * (1.0 / MLA_QK) + EPS)
        k_ref[h, :, 0:MLA_NOPE] = (knh * inv_rms * gkn).astype(BF16)
        k_ref[h, :, MLA_NOPE:] = (k_rope * inv_rms).astype(BF16)


def _mla_prep(cq, ckv, krx, pos_row, w, layer, bsz, seq):
    tm = ATTN_BLOCK
    nt = seq // tm
    per_q = FLASH_Q_BLOCK // tm
    consts = (w["g_q_lora"], w["w_uqT"], w["g_kv_lora"], w["w_uk"], w["w_uvT"],
              w["g_q_col"], w["g_k_nope"], w["g_k_rope_col"])
    in_specs = [
        pl.BlockSpec((None, tm, MLA_Q_LORA), lambda b, i: (b, i, 0)),
        pl.BlockSpec((None, tm, MLA_KV_LORA), lambda b, i: (b, i, 0)),
        pl.BlockSpec((None, tm, LANES), lambda b, i: (b, i, 0)),
        pl.BlockSpec((None, 1, tm), lambda b, i: (b, 0, i)),
        _resident(w["rope_inv"]),
    ] + [_resident(x, layer) for x in consts]
    out_specs = [
        pl.BlockSpec((None, MLA_HEADS, None, MLA_QK_PAD, tm), lambda b, i: (b, 0, i // per_q, 0, i % per_q)),
        pl.BlockSpec((None, MLA_HEADS, tm, MLA_QK_PAD), lambda b, i: (b, 0, i, 0)),
        pl.BlockSpec((None, MLA_HEADS, None, MLA_V_AUG, tm), lambda b, i: (b, 0, i, 0, 0)),
    ]
    out_shape = [
        jax.ShapeDtypeStruct((bsz, MLA_HEADS, seq // FLASH_Q_BLOCK, MLA_QK_PAD, FLASH_Q_BLOCK), BF16),
        jax.ShapeDtypeStruct((bsz, MLA_HEADS, seq, MLA_QK_PAD), BF16),
        jax.ShapeDtypeStruct((bsz, MLA_HEADS, nt, MLA_V_AUG, tm), BF16),
    ]
    return pl.pallas_call(
        _mla_prep_kernel,
        grid=(bsz, nt),
        in_specs=in_specs,
        out_specs=out_specs,
        out_shape=out_shape,
        compiler_params=_params(2),
        name="mla_prep",
    )(cq, ckv, krx, pos_row, w["rope_inv"], *consts)


def _flash_kernel(qT_ref, k_ref, vT_ref, o_ref, s_ref, cmax_ref, m_ref, acc_ref):
    bq, bk = FLASH_Q_BLOCK, ATTN_BLOCK
    kpq = bq // bk
    nq = qT_ref.shape[0]

    def scores(qi, j, slot):
        k_blk = k_ref[pl.ds(pl.multiple_of(j * bk, bk), bk), :]
        sT = jnp.dot(k_blk, qT_ref[qi], preferred_element_type=F32)
        s_ref[slot] = sT
        cmax_ref[slot] = jnp.max(sT, axis=0, keepdims=True)

    def softmax_pv(j, sT, cmax, qs):
        m_prev = m_ref[:, qs]
        m_new = jnp.maximum(m_prev, cmax)
        alpha = jnp.exp2(m_prev - m_new)
        p = jnp.exp2(sT - m_new)
        pv = jnp.dot(vT_ref[j], p.astype(BF16), preferred_element_type=F32)
        acc_ref[:, qs] = alpha * acc_ref[:, qs] + pv
        m_ref[:, qs] = m_new

    def consume(j, slot):
        softmax_pv(j, s_ref[slot], cmax_ref[slot], slice(None))

    def query_block(i, carry):
        m_ref[...] = jnp.full(m_ref.shape, -jnp.inf, F32)
        acc_ref[...] = jnp.zeros(acc_ref.shape, F32)

        def two_blocks(j):
            scores(i, j + 1, 1)
            consume(j, 0)
            scores(i, j + 2, 0)
            consume(j + 1, 1)

        def four_blocks(t, c):
            two_blocks(4 * t)
            two_blocks(4 * t + 2)
            return c

        fours = lax.shift_right_logical(i, 1)
        lax.fori_loop(0, fours, four_blocks, 0)

        @pl.when((i & 1) == 1)
        def _():
            two_blocks(4 * fours)

        d0 = i * kpq
        upper = slice(bk, bq)
        k_hi = k_ref[pl.ds(pl.multiple_of((d0 + 1) * bk, bk), bk), :]
        s_hi = jnp.dot(k_hi, qT_ref[i, :, upper], preferred_element_type=F32)
        kpos = lax.broadcasted_iota(jnp.int32, (bk, bk), 0)
        qpos = lax.broadcasted_iota(jnp.int32, (bk, bk), 1)
        tri = kpos <= qpos
        s_lo = s_ref[0]
        s_lo = jnp.concatenate([jnp.where(tri, s_lo[:, :bk], -jnp.inf), s_lo[:, upper]], axis=1)
        softmax_pv(d0, s_lo, jnp.max(s_lo, axis=0, keepdims=True), slice(None))
        scores(jnp.minimum(i + 1, nq - 1), 0, 0)
        s_hi = jnp.where(tri, s_hi, -jnp.inf)
        softmax_pv(d0 + 1, s_hi, jnp.max(s_hi, axis=0, keepdims=True), upper)
        out = acc_ref[0:MLA_V, :] / acc_ref[MLA_V:MLA_V + 1, :]
        o_ref[pl.ds(pl.multiple_of(i * bq, bq), bq), :] = out.T.astype(BF16)
        return carry

    scores(0, 0, 0)
    lax.fori_loop(0, nq, query_block, 0)


def _flash(qT, k, vT, bsz, seq):
    bq, bk = FLASH_Q_BLOCK, ATTN_BLOCK
    assert bq == 2 * bk and seq % bq == 0
    nq = seq // bq
    return pl.pallas_call(
        _flash_kernel,
        grid=(bsz, MLA_HEADS),
        in_specs=[
            pl.BlockSpec((None, None, nq, MLA_QK_PAD, bq), lambda b, h: (b, h, 0, 0, 0)),
            pl.BlockSpec((None, None, seq, MLA_QK_PAD), lambda b, h: (b, h, 0, 0)),
            pl.BlockSpec((None, None, seq // bk, MLA_V_AUG, bk), lambda b, h: (b, h, 0, 0, 0)),
        ],
        out_specs=pl.BlockSpec((None, seq, MLA_V), lambda b, h: (b, 0, h)),
        out_shape=jax.ShapeDtypeStruct((bsz, seq, MLA_HEADS * MLA_V), BF16),
        scratch_shapes=[
            pltpu.VMEM((2, bk, bq), F32),
            pltpu.VMEM((2, 1, bq), F32),
            pltpu.VMEM((1, bq), F32),
            pltpu.VMEM((MLA_V_AUG, bq), F32),
        ],
        compiler_params=_params(2),
        name="flash",
    )(qT, k, vT)


def _softplus(x):
    return jnp.maximum(x, 0.0) + jnp.log1p(jnp.exp(-jnp.abs(x)))


def _ssd_kernel(xbc_ref, dt_ref, dtT_ref, shift_ref, e4_ref, convw_ref, convb_ref, dtb_ref, dtbT_ref, alog_ref,
                alogT_ref, dskip_ref, y_ref, state_ref, halo_ref):
    step_len = xbc_ref.shape[0]
    cl = SSD_CHUNK

    @pl.when(pl.program_id(1) == 0)
    def _():
        state_ref[...] = jnp.zeros(state_ref.shape, F32)
        halo_ref[...] = jnp.zeros(halo_ref.shape, F32)

    def conv_silu(r0, tail):
        taps = SSD_CONV - 1
        xh = xbc_ref[r0:r0 + cl, :]
        xf = xh.astype(F32)
        delayed = jnp.dot(shift_ref[...], xh, preferred_element_type=F32)
        delayed = delayed.reshape(cl // CONV_HALO, taps, CONV_HALO, SSD_CONV_DIM)
        conv = convb_ref[...] + convw_ref[taps:SSD_CONV, :] * xf
        for k in range(taps):
            conv = conv + convw_ref[k:k + 1, :] * delayed[:, k].reshape(cl, SSD_CONV_DIM)
        head_in = jnp.concatenate([tail, xf[:CONV_HALO]], axis=0)
        head = convb_ref[...] + convw_ref[taps:SSD_CONV, :] * xf[:CONV_HALO]
        for k in range(taps):
            start = CONV_HALO - taps + k
            head = head + convw_ref[k:k + 1, :] * head_in[start:start + CONV_HALO]
        return _silu(jnp.concatenate([head, conv[CONV_HALO:]], axis=0)), xf[cl - CONV_HALO:]

    tail = halo_ref[...]
    xcs = []
    for c in range(step_len // cl):
        xc_c, tail = conv_silu(c * cl, tail)
        xcs.append(xc_c)
    halo_ref[...] = tail

    dt_all = _softplus(dt_ref[...] + dtb_ref[...])
    dtT_all = _softplus(dtT_ref[...] + dtbT_ref[...])
    a_row = -jnp.exp(alog_ref[...])
    a_col = -jnp.exp(alogT_ref[...])

    row = lax.broadcasted_iota(jnp.int32, (cl, cl), 0)
    col = lax.broadcasted_iota(jnp.int32, (cl, cl), 1)
    causal = col <= row
    tril = causal.astype(F32)
    triu = (row <= col).astype(F32)
    lane = lax.broadcasted_iota(jnp.int32, (cl, LANES), 1)
    left_head = lane < SSD_HEAD_DIM
    odd_copy = (lane & SSD_HEADS) != 0

    for c in range(step_len // cl):
        r0 = c * cl
        xc = xcs[c]
        dt = dt_all[r0:r0 + cl]
        dtT = dtT_all[:, r0:r0 + cl]
        acum = jnp.dot(tril, dt * a_row, preferred_element_type=F32, precision=lax.Precision.HIGHEST)
        acumT = jnp.dot(dtT * a_col, triu, preferred_element_type=F32, precision=lax.Precision.HIGHEST)
        acumT_dt = acumT - jnp.log(dtT)
        last = acum[cl - 1:cl, :]
        fac = jnp.where(odd_copy, dt * jnp.exp(last - acum), jnp.exp(acum))
        hi = fac.astype(BF16).astype(F32)
        split = jnp.where(lane < 2 * SSD_HEADS, hi, fac - hi).astype(BF16)

        for g in range(SSD_GROUPS):
            gw = SSD_HPG * SSD_HEAD_DIM
            gsl = slice(g * gw, (g + 1) * gw)
            bg = xc[:, SSD_D_INNER + g * SSD_STATE:SSD_D_INNER + (g + 1) * SSD_STATE].astype(BF16)
            cg = xc[:, SSD_D_INNER + SSD_BC + g * SSD_STATE:
                    SSD_D_INNER + SSD_BC + (g + 1) * SSD_STATE].astype(BF16)
            xs = xc[:, gsl]
            xs_b = xs.astype(BF16)
            spread = jnp.dot(split, e4_ref[g], preferred_element_type=F32)
            y_off_scale = spread[:, :gw]
            xw = (xs * spread[:, gw:]).astype(BF16)
            state_decay = y_off_scale[cl - 1:cl, :]
            cb = lax.dot_general(cg, bg, NT_DIMS, preferred_element_type=F32)
            st = state_ref[:, gsl]
            y_off = jnp.dot(cg, st.astype(BF16), preferred_element_type=F32) * y_off_scale
            for j in range(SSD_HPG // 2):
                mats = []
                for h in (g * SSD_HPG + 2 * j, g * SSD_HPG + 2 * j + 1):
                    seg = acum[:, h:h + 1] - acumT_dt[h:h + 1, :]
                    mats.append(cb * jnp.exp(jnp.where(causal, seg, -jnp.inf)))
                lhs = jnp.concatenate(mats, axis=1).astype(BF16)
                lsl = slice(2 * j * SSD_HEAD_DIM, (2 * j + 2) * SSD_HEAD_DIM)
                xp = xs_b[:, lsl]
                zero = jnp.zeros_like(xp)
                rhs = jnp.concatenate([jnp.where(left_head, xp, zero), jnp.where(left_head, zero, xp)],
                                      axis=0)
                y_diag = jnp.dot(lhs, rhs, preferred_element_type=F32)
                csl = slice(g * gw + 2 * j * SSD_HEAD_DIM, g * gw + (2 * j + 2) * SSD_HEAD_DIM)
                y = y_diag + y_off[:, lsl] + xs[:, lsl] * dskip_ref[:, csl]
                y_ref[r0:r0 + cl, csl] = y.astype(BF16)
            state_ref[:, gsl] = st * state_decay + lax.dot_general(bg, xw, TN_DIMS, preferred_element_type=F32)


def _ssd(xbc, dt, dtT, w, layer, bsz, seq):
    ls = SSD_STEP
    shared = (w["conv_shift"], w["head_spread"])
    consts = (w["conv_w"], w["conv_b"], w["dt_bias4"], w["dt_bias_col"], w["a_log4"], w["a_log_col"], w["d_skip_x"])
    return pl.pallas_call(
        _ssd_kernel,
        grid=(bsz, seq // ls),
        in_specs=[
            pl.BlockSpec((None, ls, SSD_CONV_DIM), lambda b, i: (b, i, 0)),
            pl.BlockSpec((None, ls, LANES), lambda b, i: (b, i, 0)),
            pl.BlockSpec((None, SSD_HEADS, ls), lambda b, i: (b, 0, i)),
        ] + [_resident(x) for x in shared] + [_resident(x, layer) for x in consts],
        out_specs=pl.BlockSpec((None, ls, SSD_D_INNER), lambda b, i: (b, i, 0)),
        out_shape=jax.ShapeDtypeStruct((bsz, seq, SSD_D_INNER), BF16),
        scratch_shapes=[pltpu.VMEM((SSD_STATE, SSD_D_INNER), F32), pltpu.VMEM((CONV_HALO, SSD_CONV_DIM), F32)],
        compiler_params=_params(2),
        name="ssd",
    )(xbc, dt, dtT, *shared, *consts)


def _merge_kernel(h_ref, y_ref, zs_ref, o_ref, gate_ref, gssd_ref, wssd_ref, wmla_ref, wo_ref, out_ref):
    yg = y_ref[...].astype(F32) * zs_ref[...].astype(F32)
    y_ssd = jnp.dot(_rms(yg, gssd_ref[...]).astype(BF16), wssd_ref[...], preferred_element_type=F32)
    y_mla = jnp.dot(o_ref[...], wmla_ref[...], preferred_element_type=F32)
    gate = gate_ref[...].astype(F32)
    merged = gate[:, :D_MODEL] * y_ssd + gate[:, D_MODEL:] * y_mla
    out_ref[...] = h_ref[...] + jnp.dot(merged.astype(BF16), wo_ref[...], preferred_element_type=F32)


def _merge(h, y, zs, o, gate, w, layer):
    t = h.shape[0]
    tm = TOKEN_TILE
    consts = (w["g_ssd"], w["w_ssd_out"], w["w_mla_out"], w["w_o"])

    def rows(width):
        return pl.BlockSpec((tm, width), lambda i: (i, 0))

    return pl.pallas_call(
        _merge_kernel,
        grid=(t // tm,),
        in_specs=[rows(D_MODEL), rows(SSD_D_INNER), rows(SSD_D_INNER), rows(MLA_HEADS * MLA_V), rows(2 * D_MODEL)]
        + [_resident(x, layer) for x in consts],
        out_specs=rows(D_MODEL),
        out_shape=jax.ShapeDtypeStruct((t, D_MODEL), F32),
        compiler_params=_params(1),
        name="merge",
    )(h, y, zs, o, gate, *consts)


def _split_cols(t, sizes):
    out, start = [], 0
    for n in sizes:
        out.append(t[..., start:start + n])
        start += n
    return out


def _conv_shift_matrix():
    taps = SSD_CONV - 1
    out_row = np.arange(taps * SSD_CHUNK)
    group, rest = np.divmod(out_row, taps * CONV_HALO)
    k, r = np.divmod(rest, CONV_HALO)
    src = group * CONV_HALO + r - (taps - k)
    return jnp.asarray(src[:, None] == np.arange(SSD_CHUNK)[None, :], dtype=BF16)


def _head_spread_matrix():
    gw = SSD_HPG * SSD_HEAD_DIM
    src = np.arange(LANES)
    dst = np.arange(2 * gw)
    slabs = []
    for g in range(SSD_GROUPS):
        same_head = (src[:, None] % SSD_HEADS) == g * SSD_HPG + (dst[None, :] % gw) // SSD_HEAD_DIM
        same_half = ((src[:, None] // SSD_HEADS) % 2) == dst[None, :] // gw
        slabs.append(same_head & same_half)
    return jnp.asarray(np.stack(slabs), dtype=BF16)


def _prepare_weights(p):
    half = MLA_ROPE // 2
    depth = p["w_in"].shape[0]
    bf16 = lambda t: t.astype(BF16)
    row = lambda t: t[:, None, :]
    col = lambda t: t[:, :, None]
    transpose = lambda t: jnp.swapaxes(t, -1, -2)
    w_z, w_xbc, w_dt, w_cq, w_ckv, w_kr, w_gate = _split_cols(
        p["w_in"], (SSD_D_INNER, SSD_CONV_DIM, SSD_HEADS, MLA_Q_LORA, MLA_KV_LORA, MLA_ROPE, 2 * D_MODEL))
    w_krx = jnp.concatenate([w_kr, w_kr[..., half:], w_kr[..., :half]], axis=-1)
    w_ukv = p["w_ukv"].reshape(depth, MLA_KV_LORA, MLA_HEADS, MLA_NOPE + MLA_V)
    w_uk = w_ukv[..., :MLA_NOPE].reshape(depth, MLA_KV_LORA, MLA_HEADS * MLA_NOPE)
    w_uv = w_ukv[..., MLA_NOPE:].reshape(depth, MLA_KV_LORA, MLA_HEADS * MLA_V)
    g_q = p["q_norm"] * float(MLA_QK ** -0.5 * np.log2(np.e))
    g_k = p["k_norm"]
    g1, g2 = g_k[:, MLA_NOPE:MLA_NOPE + half], g_k[:, MLA_NOPE + half:]
    rope_inv = 1.0 / (ROPE_THETA ** (jnp.arange(0, MLA_ROPE, 2, dtype=F32) / MLA_ROPE))
    copies = LANES // SSD_HEADS
    return {
        "g_ffn1": row(p["ln_ffn1"]),
        "ffn1_w13": bf16(p["ffn1_w13"]),
        "ffn1_w2": bf16(p["ffn1_w2"]),
        "g_ffn2": row(p["ln_ffn2"]),
        "ffn2_w13": bf16(p["ffn2_w13"]),
        "ffn2_w2": bf16(p["ffn2_w2"]),
        "g_mix": row(p["ln_mix"]),
        "w_z": bf16(w_z),
        "w_xbc": bf16(w_xbc),
        "w_dt4_krx": bf16(jnp.concatenate([jnp.tile(w_dt, (1, 1, copies)), w_krx], axis=-1)),
        "w_dtT": bf16(transpose(w_dt)),
        "w_cq": bf16(w_cq),
        "w_ckv": bf16(w_ckv),
        "w_gate": bf16(w_gate),
        "conv_w": p["conv_w"],
        "conv_b": row(p["conv_b"]),
        "dt_bias4": row(jnp.tile(p["dt_bias"], (1, copies))),
        "dt_bias_col": col(p["dt_bias"]),
        "a_log4": row(jnp.tile(p["a_log"], (1, copies))),
        "a_log_col": col(p["a_log"]),
        "d_skip_x": row(jnp.repeat(p["d_skip"], SSD_HEAD_DIM, axis=-1)),
        "g_ssd": row(p["ssd_norm"]),
        "w_ssd_out": bf16(p["w_ssd_out"]),
        "g_q_lora": row(p["q_lora_norm"]),
        "w_uqT": bf16(transpose(p["w_uq"])),
        "g_kv_lora": row(p["kv_lora_norm"]),
        "w_uk": bf16(w_uk),
        "w_uvT": bf16(transpose(w_uv)),
        "g_q_col": col(g_q),
        "g_k_nope": row(g_k[:, :MLA_NOPE]),
        "g_k_rope_col": col(jnp.concatenate([g1, g2, -g2, g1], axis=-1)),
        "w_mla_out": bf16(p["w_mla_out"]),
        "w_o": bf16(p["w_o"]),
        "head_spread": _head_spread_matrix(),
        "conv_shift": _conv_shift_matrix(),
        "rope_inv": rope_inv[:, None],
    }


def kernel(x, positions, ln_ffn1, ffn1_w13, ffn1_w2, ln_mix, w_in, conv_w, conv_b, dt_bias, a_log, d_skip,
           ssd_norm, w_ssd_out, q_lora_norm, w_uq, kv_lora_norm, w_ukv, q_norm, k_norm, w_mla_out, w_o,
           ln_ffn2, ffn2_w13, ffn2_w2):
    bsz, seq, d_model = x.shape
    assert d_model == D_MODEL and seq % FLASH_Q_BLOCK == 0 and seq % SSD_STEP == 0
    assert seq % TOKEN_TILE == 0 and seq % IN_PROJ_TILE == 0
    params = dict(ln_ffn1=ln_ffn1, ffn1_w13=ffn1_w13, ffn1_w2=ffn1_w2, ln_mix=ln_mix, w_in=w_in, conv_w=conv_w,
                  conv_b=conv_b, dt_bias=dt_bias, a_log=a_log, d_skip=d_skip, ssd_norm=ssd_norm,
                  w_ssd_out=w_ssd_out, q_lora_norm=q_lora_norm, w_uq=w_uq, kv_lora_norm=kv_lora_norm, w_ukv=w_ukv,
                  q_norm=q_norm, k_norm=k_norm, w_mla_out=w_mla_out, w_o=w_o, ln_ffn2=ln_ffn2, ffn2_w13=ffn2_w13,
                  ffn2_w2=ffn2_w2)
    tokens = bsz * seq
    pos_row = positions.reshape(bsz, 1, seq)
    h = x.reshape(tokens, D_MODEL)
    w = _prepare_weights(params)
    for layer in range(ln_ffn1.shape[0]):
        h = _ffn(h, w["g_ffn1"], w["ffn1_w13"], w["ffn1_w2"], layer)
        zs, xbc, dt, dtT, cq, ckv, krx, gate = _in_proj(h, w, layer, bsz, seq)
        y = _ssd(xbc, dt, dtT, w, layer, bsz, seq)
        qT, k, vT = _mla_prep(cq, ckv, krx, pos_row, w, layer, bsz, seq)
        o = _flash(qT, k, vT, bsz, seq)
        h = _merge(h, y.reshape(tokens, SSD_D_INNER), zs.reshape(tokens, SSD_D_INNER),
                   o.reshape(tokens, MLA_HEADS * MLA_V), gate.reshape(tokens, 2 * D_MODEL), w, layer)
        h = _ffn(h, w["g_ffn2"], w["ffn2_w13"], w["ffn2_w2"], layer)
    return h.reshape(bsz, seq, D_MODEL)
```

```python
import jax
import jax.numpy as jnp
import numpy as np
from jax import lax
from jax.experimental import pallas as pl
from jax.experimental.pallas import tpu as pltpu

F32 = jnp.float32
BF16 = jnp.bfloat16

D_MODEL = 1024
D_FF = 2816
SSD_D_INNER = 2048
SSD_HEAD_DIM = 64
SSD_HEADS = 32
SSD_GROUPS = 4
SSD_HPG = 8
SSD_STATE = 128
SSD_CONV = 4
SSD_CHUNK = 128
SSD_BC = SSD_GROUPS * SSD_STATE
SSD_CONV_DIM = SSD_D_INNER + 2 * SSD_BC
MLA_HEADS = 8
MLA_Q_LORA = 512
MLA_KV_LORA = 256
MLA_NOPE = 128
MLA_ROPE = 64
MLA_V = 128
MLA_QK = MLA_NOPE + MLA_ROPE
MLA_V_AUG = MLA_V + 16
MLA_QK_PAD = 256
ROPE_THETA = 10000.0
EPS = 1e-6

LANES = 128
VMEM_LIMIT_BYTES = 56 * 1024 * 1024

TOKEN_TILE = 512
IN_PROJ_TILE = 512
SSD_STEP = 512
ATTN_BLOCK = 512
FLASH_Q_BLOCK = 1024
CONV_HALO = 8

NT_DIMS = (((1,), (1,)), ((), ()))
TN_DIMS = (((0,), (0,)), ((), ()))


def _params(n_grid_axes):
    return pltpu.CompilerParams(
        dimension_semantics=("arbitrary",) * n_grid_axes,
        vmem_limit_bytes=VMEM_LIMIT_BYTES,
    )


def _resident(x, layer=None):
    if layer is None:
        zeros = (0,) * x.ndim
        return pl.BlockSpec(x.shape, lambda *_: zeros, pipeline_mode=pl.Buffered(1))
    index = (layer,) + (0,) * (x.ndim - 1)
    return pl.BlockSpec((None,) + x.shape[1:], lambda *_: index, pipeline_mode=pl.Buffered(1))


def _rms(x, gain):
    return x * lax.rsqrt(jnp.mean(x * x, axis=-1, keepdims=True) + EPS) * gain


def _silu(x):
    return x * jax.nn.sigmoid(x)


def _ffn_kernel(h_ref, g_ref, w13_ref, w2_ref, o_ref):
    x = h_ref[...]
    xb = _rms(x, g_ref[...]).astype(BF16)
    gate = jnp.dot(xb, w13_ref[:, :D_FF], preferred_element_type=F32)
    up = jnp.dot(xb, w13_ref[:, D_FF:], preferred_element_type=F32)
    hidden = (_silu(gate) * up).astype(BF16)
    o_ref[...] = x + 0.5 * jnp.dot(hidden, w2_ref[...], preferred_element_type=F32)


def _ffn(h, gain, w13, w2, layer):
    t = h.shape[0]
    return pl.pallas_call(
        _ffn_kernel,
        grid=(t // TOKEN_TILE,),
        in_specs=[
            pl.BlockSpec((TOKEN_TILE, D_MODEL), lambda i: (i, 0)),
            _resident(gain, layer),
            _resident(w13, layer),
            _resident(w2, layer),
        ],
        out_specs=pl.BlockSpec((TOKEN_TILE, D_MODEL), lambda i: (i, 0)),
        out_shape=jax.ShapeDtypeStruct((t, D_MODEL), F32),
        compiler_params=_params(1),
        name="ffn",
    )(h, gain, w13, w2)


def _in_proj_kernel(h_ref, g_ref, wz_ref, wxbc_ref, wdtkr_ref, wdtT_ref, wcq_ref, wckv_ref, wg_ref,
                    zs_ref, xbc_ref, dt_ref, dtT_ref, cq_ref, ckv_ref, krx_ref, gate_ref):
    xb = _rms(h_ref[...], g_ref[...]).astype(BF16)

    def proj(w_ref):
        return jnp.dot(xb, w_ref[...], preferred_element_type=F32)

    zs_ref[...] = _silu(proj(wz_ref)).astype(BF16)
    gate_ref[...] = jax.nn.sigmoid(proj(wg_ref)).astype(BF16)

    xbc_ref[...] = proj(wxbc_ref).astype(BF16)
    dt_krx = proj(wdtkr_ref)
    dt_ref[...] = dt_krx[:, :LANES]
    krx_ref[...] = dt_krx[:, LANES:]
    dtT_ref[...] = lax.dot_general(wdtT_ref[...], xb, NT_DIMS, preferred_element_type=F32)
    cq_ref[...] = proj(wcq_ref).astype(BF16)
    ckv_ref[...] = proj(wckv_ref).astype(BF16)


def _in_proj(h, w, layer, bsz, seq):
    tm = IN_PROJ_TILE
    nt = seq // tm
    widths = (SSD_D_INNER, SSD_CONV_DIM, LANES, None, MLA_Q_LORA, MLA_KV_LORA, LANES, 2 * D_MODEL)
    dtypes = (BF16, BF16, F32, F32, BF16, BF16, F32, BF16)
    out_specs, out_shape = [], []
    for width, dtype in zip(widths, dtypes):
        if width is None:
            out_specs.append(pl.BlockSpec((None, SSD_HEADS, tm), lambda b, i: (b, 0, i)))
            out_shape.append(jax.ShapeDtypeStruct((bsz, SSD_HEADS, seq), dtype))
        else:
            out_specs.append(pl.BlockSpec((None, tm, width), lambda b, i: (b, i, 0)))
            out_shape.append(jax.ShapeDtypeStruct((bsz, seq, width), dtype))
    weights = (w["w_z"], w["w_xbc"], w["w_dt4_krx"], w["w_dtT"], w["w_cq"], w["w_ckv"], w["w_gate"])
    return pl.pallas_call(
        _in_proj_kernel,
        grid=(bsz, nt),
        in_specs=[pl.BlockSpec((None, tm, D_MODEL), lambda b, i: (b, i, 0)), _resident(w["g_mix"], layer)]
        + [_resident(x, layer) for x in weights],
        out_specs=out_specs,
        out_shape=out_shape,
        compiler_params=_params(2),
        name="in_proj",
    )(h.reshape(bsz, seq, D_MODEL), w["g_mix"], *weights)


def _mla_prep_kernel(cq_ref, ckv_ref, krx_ref, pos_ref, inv_ref, gql_ref, wuqT_ref, gkvl_ref, wuk_ref, wuvT_ref,
                     gq_ref, gkn_ref, gkr_ref, qT_ref, k_ref, vT_ref):
    tm = cq_ref.shape[0]
    half = MLA_ROPE // 2
    ang = inv_ref[...] * pos_ref[...].astype(F32)
    cos, sin = jnp.cos(ang), jnp.sin(ang)

    qn = _rms(cq_ref[...].astype(F32), gql_ref[...]).astype(BF16)
    qT = lax.dot_general(wuqT_ref[...], qn, NT_DIMS, preferred_element_type=F32)
    gq = jnp.broadcast_to(gq_ref[...], (MLA_QK, tm))
    for h in range(MLA_HEADS):
        blk = qT[h * MLA_QK:(h + 1) * MLA_QK]
        inv_rms = lax.rsqrt(jnp.sum(blk * blk, axis=0, keepdims=True) * (1.0 / MLA_QK) + EPS)
        xn = blk * inv_rms * gq
        t1 = xn[MLA_NOPE:MLA_NOPE + half]
        t2 = xn[MLA_NOPE + half:]
        qT_ref[h, 0:MLA_NOPE, :] = xn[:MLA_NOPE].astype(BF16)
        qT_ref[h, MLA_NOPE:MLA_NOPE + half, :] = (t1 * cos - t2 * sin).astype(BF16)
        qT_ref[h, MLA_NOPE + half:MLA_QK, :] = (t2 * cos + t1 * sin).astype(BF16)
        qT_ref[h, MLA_QK:, :] = jnp.zeros((MLA_QK_PAD - MLA_QK, tm), BF16)

    kvn = _rms(ckv_ref[...].astype(F32), gkvl_ref[...]).astype(BF16)
    kn = jnp.dot(kvn, wuk_ref[...], preferred_element_type=F32)
    vT = lax.dot_general(wuvT_ref[...], kvn, NT_DIMS, preferred_element_type=F32)
    vT_ref[:, 0:MLA_V, :] = vT.reshape(MLA_HEADS, MLA_V, tm).astype(BF16)
    ones_row = (lax.broadcasted_iota(jnp.int32, (MLA_HEADS, MLA_V_AUG - MLA_V, tm), 1) == 0).astype(BF16)
    vT_ref[:, MLA_V:, :] = ones_row

    krx = krx_ref[...]
    rope_ss = 0.5 * jnp.sum(krx * krx, axis=-1, keepdims=True)
    table = (jnp.concatenate([cos, cos, sin, sin], axis=0) * gkr_ref[...]).T
    prod = krx * table
    k_rope = prod + pltpu.roll(prod, MLA_ROPE, 1)
    gkn = gkn_ref[...]
    for h in range(MLA_HEADS):
        knh = kn[:, h * MLA_NOPE:(h + 1) * MLA_NOPE]
        ss = jnp.sum(knh * knh, axis=-1, keepdims=True) + rope_ss
        inv_rms = lax.rsqrt(ss * (1.0 / MLA_QK) + EPS)
        k_ref[h, :, 0:MLA_NOPE] = (knh * inv_rms * gkn).astype(BF16)
        k_ref[h, :, MLA_NOPE:] = (k_rope * inv_rms).astype(BF16)


def _mla_prep(cq, ckv, krx, pos_row, w, layer, bsz, seq):
    tm = ATTN_BLOCK
    nt = seq // tm
    per_q = FLASH_Q_BLOCK // tm
    consts = (w["g_q_lora"], w["w_uqT"], w["g_kv_lora"], w["w_uk"], w["w_uvT"],
              w["g_q_col"], w["g_k_nope"], w["g_k_rope_col"])
    in_specs = [
        pl.BlockSpec((None, tm, MLA_Q_LORA), lambda b, i: (b, i, 0)),
        pl.BlockSpec((None, tm, MLA_KV_LORA), lambda b, i: (b, i, 0)),
        pl.BlockSpec((None, tm, LANES), lambda b, i: (b, i, 0)),
        pl.BlockSpec((None, 1, tm), lambda b, i: (b, 0, i)),
        _resident(w["rope_inv"]),
    ] + [_resident(x, layer) for x in consts]
    out_specs = [
        pl.BlockSpec((None, MLA_HEADS, None, MLA_QK_PAD, tm), lambda b, i: (b, 0, i // per_q, 0, i % per_q)),
        pl.BlockSpec((None, MLA_HEADS, tm, MLA_QK_PAD), lambda b, i: (b, 0, i, 0)),
        pl.BlockSpec((None, MLA_HEADS, None, MLA_V_AUG, tm), lambda b, i: (b, 0, i, 0, 0)),
    ]
    out_shape = [
        jax.ShapeDtypeStruct((bsz, MLA_HEADS, seq // FLASH_Q_BLOCK, MLA_QK_PAD, FLASH_Q_BLOCK), BF16),
        jax.ShapeDtypeStruct((bsz, MLA_HEADS, seq, MLA_QK_PAD), BF16),
        jax.ShapeDtypeStruct((bsz, MLA_HEADS, nt, MLA_V_AUG, tm), BF16),
    ]
    return pl.pallas_call(
        _mla_prep_kernel,
        grid=(bsz, nt),
        in_specs=in_specs,
        out_specs=out_specs,
        out_shape=out_shape,
        compiler_params=_params(2),
        name="mla_prep",
    )(cq, ckv, krx, pos_row, w["rope_inv"], *consts)


def _flash_kernel(qT_ref, k_ref, vT_ref, o_ref, s_ref, cmax_ref, m_ref, acc_ref):
    bq, bk = FLASH_Q_BLOCK, ATTN_BLOCK
    kpq = bq // bk
    nq = qT_ref.shape[0]

    def scores(qi, j, slot):
        k_blk = k_ref[pl.ds(pl.multiple_of(j * bk, bk), bk), :]
        sT = jnp.dot(k_blk, qT_ref[qi], preferred_element_type=F32)
        s_ref[slot] = sT
        cmax_ref[slot] = jnp.max(sT, axis=0, keepdims=True)

    def softmax_pv(j, sT, cmax, qs):
        m_prev = m_ref[:, qs]
        m_new = jnp.maximum(m_prev, cmax)
        alpha = jnp.exp2(m_prev - m_new)
        p = jnp.exp2(sT - m_new)
        pv = jnp.dot(vT_ref[j], p.astype(BF16), preferred_element_type=F32)
        acc_ref[:, qs] = alpha * acc_ref[:, qs] + pv
        m_ref[:, qs] = m_new

    def consume(j, slot):
        softmax_pv(j, s_ref[slot], cmax_ref[slot], slice(None))

    def query_block(i, carry):
        m_ref[...] = jnp.full(m_ref.shape, -jnp.inf, F32)
        acc_ref[...] = jnp.zeros(acc_ref.shape, F32)

        def two_blocks(j):
            scores(i, j + 1, 1)
            consume(j, 0)
            scores(i, j + 2, 0)
            consume(j + 1, 1)

        def four_blocks(t, c):
            two_blocks(4 * t)
            two_blocks(4 * t + 2)
            return c

        fours = lax.shift_right_logical(i, 1)
        lax.fori_loop(0, fours, four_blocks, 0)

        @pl.when((i & 1) == 1)
        def _():
            two_blocks(4 * fours)

        d0 = i * kpq
        upper = slice(bk, bq)
        k_hi = k_ref[pl.ds(pl.multiple_of((d0 + 1) * bk, bk), bk), :]
        s_hi = jnp.dot(k_hi, qT_ref[i, :, upper], preferred_element_type=F32)
        kpos = lax.broadcasted_iota(jnp.int32, (bk, bk), 0)
        qpos = lax.broadcasted_iota(jnp.int32, (bk, bk), 1)
        tri = kpos <= qpos
        s_lo = s_ref[0]
        s_lo = jnp.concatenate([jnp.where(tri, s_lo[:, :bk], -jnp.inf), s_lo[:, upper]], axis=1)
        softmax_pv(d0, s_lo, jnp.max(s_lo, axis=0, keepdims=True), slice(None))
        scores(jnp.minimum(i + 1, nq - 1), 0, 0)
        s_hi = jnp.where(tri, s_hi, -jnp.inf)
        softmax_pv(d0 + 1, s_hi, jnp.max(s_hi, axis=0, keepdims=True), upper)
        out = acc_ref[0:MLA_V, :] / acc_ref[MLA_V:MLA_V + 1, :]
        o_ref[pl.ds(pl.multiple_of(i * bq, bq), bq), :] = out.T.astype(BF16)
        return carry

    scores(0, 0, 0)
    lax.fori_loop(0, nq, query_block, 0)


def _flash(qT, k, vT, bsz, seq):
    bq, bk = FLASH_Q_BLOCK, ATTN_BLOCK
    assert bq == 2 * bk and seq % bq == 0
    nq = seq // bq
    return pl.pallas_call(
        _flash_kernel,
        grid=(bsz, MLA_HEADS),
        in_specs=[
            pl.BlockSpec((None, None, nq, MLA_QK_PAD, bq), lambda b, h: (b, h, 0, 0, 0)),
            pl.BlockSpec((None, None, seq, MLA_QK_PAD), lambda b, h: (b, h, 0, 0)),
            pl.BlockSpec((None, None, seq // bk, MLA_V_AUG, bk), lambda b, h: (b, h, 0, 0, 0)),
        ],
        out_specs=pl.BlockSpec((None, seq, MLA_V), lambda b, h: (b, 0, h)),
        out_shape=jax.ShapeDtypeStruct((bsz, seq, MLA_HEADS * MLA_V), BF16),
        scratch_shapes=[
            pltpu.VMEM((2, bk, bq), F32),
            pltpu.VMEM((2, 1, bq), F32),
            pltpu.VMEM((1, bq), F32),
            pltpu.VMEM((MLA_V_AUG, bq), F32),
        ],
        compiler_params=_params(2),
        name="flash",
    )(qT, k, vT)


def _softplus(x):
    return jnp.maximum(x, 0.0) + jnp.log1p(jnp.exp(-jnp.abs(x)))


def _ssd_kernel(xbc_ref, dt_ref, dtT_ref, shift_ref, e4_ref, convw_ref, convb_ref, dtb_ref, dtbT_ref, alog_ref,
                alogT_ref, dskip_ref, y_ref, state_ref, halo_ref):
    step_len = xbc_ref.shape[0]
    cl = SSD_CHUNK

    @pl.when(pl.program_id(1) == 0)
    def _():
        state_ref[...] = jnp.zeros(state_ref.shape, F32)
        halo_ref[...] = jnp.zeros(halo_ref.shape, F32)

    def conv_silu(r0, tail):
        taps = SSD_CONV - 1
        xh = xbc_ref[r0:r0 + cl, :]
        xf = xh.astype(F32)
        delayed = jnp.dot(shift_ref[...], xh, preferred_element_type=F32)
        delayed = delayed.reshape(cl // CONV_HALO, taps, CONV_HALO, SSD_CONV_DIM)
        conv = convb_ref[...] + convw_ref[taps:SSD_CONV, :] * xf
        for k in range(taps):
            conv = conv + convw_ref[k:k + 1, :] * delayed[:, k].reshape(cl, SSD_CONV_DIM)
        head_in = jnp.concatenate([tail, xf[:CONV_HALO]], axis=0)
        head = convb_ref[...] + convw_ref[taps:SSD_CONV, :] * xf[:CONV_HALO]
        for k in range(taps):
            start = CONV_HALO - taps + k
            head = head + convw_ref[k:k + 1, :] * head_in[start:start + CONV_HALO]
        return _silu(jnp.concatenate([head, conv[CONV_HALO:]], axis=0)), xf[cl - CONV_HALO:]

    tail = halo_ref[...]
    xcs = []
    for c in range(step_len // cl):
        xc_c, tail = conv_silu(c * cl, tail)
        xcs.append(xc_c)
    halo_ref[...] = tail

    dt_all = _softplus(dt_ref[...] + dtb_ref[...])
    dtT_all = _softplus(dtT_ref[...] + dtbT_ref[...])
    a_row = -jnp.exp(alog_ref[...])
    a_col = -jnp.exp(alogT_ref[...])

    row = lax.broadcasted_iota(jnp.int32, (cl, cl), 0)
    col = lax.broadcasted_iota(jnp.int32, (cl, cl), 1)
    causal = col <= row
    tril = causal.astype(F32)
    triu = (row <= col).astype(F32)
    lane = lax.broadcasted_iota(jnp.int32, (cl, LANES), 1)
    left_head = lane < SSD_HEAD_DIM
    odd_copy = (lane & SSD_HEADS) != 0

    for c in range(step_len // cl):
        r0 = c * cl
        xc = xcs[c]
        dt = dt_all[r0:r0 + cl]
        dtT = dtT_all[:, r0:r0 + cl]
        acum = jnp.dot(tril, dt * a_row, preferred_element_type=F32, precision=lax.Precision.HIGHEST)
        acumT = jnp.dot(dtT * a_col, triu, preferred_element_type=F32, precision=lax.Precision.HIGHEST)
        acumT_dt = acumT - jnp.log(dtT)
        last = acum[cl - 1:cl, :]
        fac = jnp.where(odd_copy, dt * jnp.exp(last - acum), jnp.exp(acum))
        hi = fac.astype(BF16).astype(F32)
        split = jnp.where(lane < 2 * SSD_HEADS, hi, fac - hi).astype(BF16)

        for g in range(SSD_GROUPS):
            gw = SSD_HPG * SSD_HEAD_DIM
            gsl = slice(g * gw, (g + 1) * gw)
            bg = xc[:, SSD_D_INNER + g * SSD_STATE:SSD_D_INNER + (g + 1) * SSD_STATE].astype(BF16)
            cg = xc[:, SSD_D_INNER + SSD_BC + g * SSD_STATE:
                    SSD_D_INNER + SSD_BC + (g + 1) * SSD_STATE].astype(BF16)
            xs = xc[:, gsl]
            xs_b = xs.astype(BF16)
            spread = jnp.dot(split, e4_ref[g], preferred_element_type=F32)
            y_off_scale = spread[:, :gw]
            xw = (xs * spread[:, gw:]).astype(BF16)
            state_decay = y_off_scale[cl - 1:cl, :]
            cb = lax.dot_general(cg, bg, NT_DIMS, preferred_element_type=F32)
            st = state_ref[:, gsl]
            y_off = jnp.dot(cg, st.astype(BF16), preferred_element_type=F32) * y_off_scale
            for j in range(SSD_HPG // 2):
                mats = []
                for h in (g * SSD_HPG + 2 * j, g * SSD_HPG + 2 * j + 1):
                    seg = acum[:, h:h + 1] - acumT_dt[h:h + 1, :]
                    mats.append(cb * jnp.exp(jnp.where(causal, seg, -jnp.inf)))
                lhs = jnp.concatenate(mats, axis=1).astype(BF16)
                lsl = slice(2 * j * SSD_HEAD_DIM, (2 * j + 2) * SSD_HEAD_DIM)
                xp = xs_b[:, lsl]
                zero = jnp.zeros_like(xp)
                rhs = jnp.concatenate([jnp.where(left_head, xp, zero), jnp.where(left_head, zero, xp)],
                                      axis=0)
                y_diag = jnp.dot(lhs, rhs, preferred_element_type=F32)
                csl = slice(g * gw + 2 * j * SSD_HEAD_DIM, g * gw + (2 * j + 2) * SSD_HEAD_DIM)
                y = y_diag + y_off[:, lsl] + xs[:, lsl] * dskip_ref[:, csl]
                y_ref[r0:r0 + cl, csl] = y.astype(BF16)
            state_ref[:, gsl] = st * state_decay + lax.dot_general(bg, xw, TN_DIMS, preferred_element_type=F32)


def _ssd(xbc, dt, dtT, w, layer, bsz, seq):
    ls = SSD_STEP
    shared = (w["conv_shift"], w["head_spread"])
    consts = (w["conv_w"], w["conv_b"], w["dt_bias4"], w["dt_bias_col"], w["a_log4"], w["a_log_col"], w["d_skip_x"])
    return pl.pallas_call(
        _ssd_kernel,
        grid=(bsz, seq // ls),
        in_specs=[
            pl.BlockSpec((None, ls, SSD_CONV_DIM), lambda b, i: (b, i, 0)),
            pl.BlockSpec((None, ls, LANES), lambda b, i: (b, i, 0)),
            pl.BlockSpec((None, SSD_HEADS, ls), lambda b, i: (b, 0, i)),
        ] + [_resident(x) for x in shared] + [_resident(x, layer) for x in consts],
        out_specs=pl.BlockSpec((None, ls, SSD_D_INNER), lambda b, i: (b, i, 0)),
        out_shape=jax.ShapeDtypeStruct((bsz, seq, SSD_D_INNER), BF16),
        scratch_shapes=[pltpu.VMEM((SSD_STATE, SSD_D_INNER), F32), pltpu.VMEM((CONV_HALO, SSD_CONV_DIM), F32)],
        compiler_params=_params(2),
        name="ssd",
    )(xbc, dt, dtT, *shared, *consts)


def _merge_kernel(h_ref, y_ref, zs_ref, o_ref, gate_ref, gssd_ref, wssd_ref, wmla_ref, wo_ref, out_ref):
    yg = y_ref[...].astype(F32) * zs_ref[...].astype(F32)
    y_ssd = jnp.dot(_rms(yg, gssd_ref[...]).astype(BF16), wssd_ref[...], preferred_element_type=F32)
    y_mla = jnp.dot(o_ref[...], wmla_ref[...], preferred_element_type=F32)
    gate = gate_ref[...].astype(F32)
    merged = gate[:, :D_MODEL] * y_ssd + gate[:, D_MODEL:] * y_mla
    out_ref[...] = h_ref[...] + jnp.dot(merged.astype(BF16), wo_ref[...], preferred_element_type=F32)


def _merge(h, y, zs, o, gate, w, layer):
    t = h.shape[0]
    tm = TOKEN_TILE
    consts = (w["g_ssd"], w["w_ssd_out"], w["w_mla_out"], w["w_o"])

    def rows(width):
        return pl.BlockSpec((tm, width), lambda i: (i, 0))

    return pl.pallas_call(
        _merge_kernel,
        grid=(t // tm,),
        in_specs=[rows(D_MODEL), rows(SSD_D_INNER), rows(SSD_D_INNER), rows(MLA_HEADS * MLA_V), rows(2 * D_MODEL)]
        + [_resident(x, layer) for x in consts],
        out_specs=rows(D_MODEL),
        out_shape=jax.ShapeDtypeStruct((t, D_MODEL), F32),
        compiler_params=_params(1),
        name="merge",
    )(h, y, zs, o, gate, *consts)


def _split_cols(t, sizes):
    out, start = [], 0
    for n in sizes:
        out.append(t[..., start:start + n])
        start += n
    return out


def _conv_shift_matrix():
    taps = SSD_CONV - 1
    out_row = np.arange(taps * SSD_CHUNK)
    group, rest = np.divmod(out_row, taps * CONV_HALO)
    k, r = np.divmod(rest, CONV_HALO)
    src = group * CONV_HALO + r - (taps - k)
    return jnp.asarray(src[:, None] == np.arange(SSD_CHUNK)[None, :], dtype=BF16)


def _head_spread_matrix():
    gw = SSD_HPG * SSD_HEAD_DIM
    src = np.arange(LANES)
    dst = np.arange(2 * gw)
    slabs = []
    for g in range(SSD_GROUPS):
        same_head = (src[:, None] % SSD_HEADS) == g * SSD_HPG + (dst[None, :] % gw) // SSD_HEAD_DIM
        same_half = ((src[:, None] // SSD_HEADS) % 2) == dst[None, :] // gw
        slabs.append(same_head & same_half)
    return jnp.asarray(np.stack(slabs), dtype=BF16)


def _prepare_weights(p):
    half = MLA_ROPE // 2
    depth = p["w_in"].shape[0]
    bf16 = lambda t: t.astype(BF16)
    row = lambda t: t[:, None, :]
    col = lambda t: t[:, :, None]
    transpose = lambda t: jnp.swapaxes(t, -1, -2)
    w_z, w_xbc, w_dt, w_cq, w_ckv, w_kr, w_gate = _split_cols(
        p["w_in"], (SSD_D_INNER, SSD_CONV_DIM, SSD_HEADS, MLA_Q_LORA, MLA_KV_LORA, MLA_ROPE, 2 * D_MODEL))
    w_krx = jnp.concatenate([w_kr, w_kr[..., half:], w_kr[..., :half]], axis=-1)
    w_ukv = p["w_ukv"].reshape(depth, MLA_KV_LORA, MLA_HEADS, MLA_NOPE + MLA_V)
    w_uk = w_ukv[..., :MLA_NOPE].reshape(depth, MLA_KV_LORA, MLA_HEADS * MLA_NOPE)
    w_uv = w_ukv[..., MLA_NOPE:].reshape(depth, MLA_KV_LORA, MLA_HEADS * MLA_V)
    g_q = p["q_norm"] * float(MLA_QK ** -0.5 * np.log2(np.e))
    g_k = p["k_norm"]
    g1, g2 = g_k[:, MLA_NOPE:MLA_NOPE + half], g_k[:, MLA_NOPE + half:]
    rope_inv = 1.0 / (ROPE_THETA ** (jnp.arange(0, MLA_ROPE, 2, dtype=F32) / MLA_ROPE))
    copies = LANES // SSD_HEADS
    return {
        "g_ffn1": row(p["ln_ffn1"]),
        "ffn1_w13": bf16(p["ffn1_w13"]),
        "ffn1_w2": bf16(p["ffn1_w2"]),
        "g_ffn2": row(p["ln_ffn2"]),
        "ffn2_w13": bf16(p["ffn2_w13"]),
        "ffn2_w2": bf16(p["ffn2_w2"]),
        "g_mix": row(p["ln_mix"]),
        "w_z": bf16(w_z),
        "w_xbc": bf16(w_xbc),
        "w_dt4_krx": bf16(jnp.concatenate([jnp.tile(w_dt, (1, 1, copies)), w_krx], axis=-1)),
        "w_dtT": bf16(transpose(w_dt)),
        "w_cq": bf16(w_cq),
        "w_ckv": bf16(w_ckv),
        "w_gate": bf16(w_gate),
        "conv_w": p["conv_w"],
        "conv_b": row(p["conv_b"]),
        "dt_bias4": row(jnp.tile(p["dt_bias"], (1, copies))),
        "dt_bias_col": col(p["dt_bias"]),
        "a_log4": row(jnp.tile(p["a_log"], (1, copies))),
        "a_log_col": col(p["a_log"]),
        "d_skip_x": row(jnp.repeat(p["d_skip"], SSD_HEAD_DIM, axis=-1)),
        "g_ssd": row(p["ssd_norm"]),
        "w_ssd_out": bf16(p["w_ssd_out"]),
        "g_q_lora": row(p["q_lora_norm"]),
        "w_uqT": bf16(transpose(p["w_uq"])),
        "g_kv_lora": row(p["kv_lora_norm"]),
        "w_uk": bf16(w_uk),
        "w_uvT": bf16(transpose(w_uv)),
        "g_q_col": col(g_q),
        "g_k_nope": row(g_k[:, :MLA_NOPE]),
        "g_k_rope_col": col(jnp.concatenate([g1, g2, -g2, g1], axis=-1)),
        "w_mla_out": bf16(p["w_mla_out"]),
        "w_o": bf16(p["w_o"]),
        "head_spread": _head_spread_matrix(),
        "conv_shift": _conv_shift_matrix(),
        "rope_inv": rope_inv[:, None],
    }


def kernel(x, positions, ln_ffn1, ffn1_w13, ffn1_w2, ln_mix, w_in, conv_w, conv_b, dt_bias, a_log, d_skip,
           ssd_norm, w_ssd_out, q_lora_norm, w_uq, kv_lora_norm, w_ukv, q_norm, k_norm, w_mla_out, w_o,
           ln_ffn2, ffn2_w13, ffn2_w2):
    bsz, seq, d_model = x.shape
    assert d_model == D_MODEL and seq % FLASH_Q_BLOCK == 0 and seq % SSD_STEP == 0
    assert seq % TOKEN_TILE == 0 and seq % IN_PROJ_TILE == 0
    params = dict(ln_ffn1=ln_ffn1, ffn1_w13=ffn1_w13, ffn1_w2=ffn1_w2, ln_mix=ln_mix, w_in=w_in, conv_w=conv_w,
                  conv_b=conv_b, dt_bias=dt_bias, a_log=a_log, d_skip=d_skip, ssd_norm=ssd_norm,
                  w_ssd_out=w_ssd_out, q_lora_norm=q_lora_norm, w_uq=w_uq, kv_lora_norm=kv_lora_norm, w_ukv=w_ukv,
                  q_norm=q_norm, k_norm=k_norm, w_mla_out=w_mla_out, w_o=w_o, ln_ffn2=ln_ffn2, ffn2_w13=ffn2_w13,
                  ffn2_w2=ffn2_w2)
    tokens = bsz * seq
    pos_row = positions.reshape(bsz, 1, seq)
    h = x.reshape(tokens, D_MODEL)
    w = _prepare_weights(params)
    for layer in range(ln_ffn1.shape[0]):
        h = _ffn(h, w["g_ffn1"], w["ffn1_w13"], w["ffn1_w2"], layer)
        zs, xbc, dt, dtT, cq, ckv, krx, gate = _in_proj(h, w, layer, bsz, seq)
        y = _ssd(xbc, dt, dtT, w, layer, bsz, seq)
        qT, k, vT = _mla_prep(cq, ckv, krx, pos_row, w, layer, bsz, seq)
        o = _flash(qT, k, vT, bsz, seq)
        h = _merge(h, y.reshape(tokens, SSD_D_INNER), zs.reshape(tokens, SSD_D_INNER),
                   o.reshape(tokens, MLA_HEADS * MLA_V), gate.reshape(tokens, 2 * D_MODEL), w, layer)
        h = _ffn(h, w["g_ffn2"], w["ffn2_w13"], w["ffn2_w2"], layer)
    return h.reshape(bsz, seq, D_MODEL)
```

```python
import jax
import jax.numpy as jnp
import numpy as np
from jax import lax
from jax.experimental import pallas as pl
from jax.experimental.pallas import tpu as pltpu

F32 = jnp.float32
BF16 = jnp.bfloat16

D_MODEL = 1024
D_FF = 2816
SSD_D_INNER = 2048
SSD_HEAD_DIM = 64
SSD_HEADS = 32
SSD_GROUPS = 4
SSD_HPG = 8
SSD_STATE = 128
SSD_CONV = 4
SSD_CHUNK = 128
SSD_BC = SSD_GROUPS * SSD_STATE
SSD_CONV_DIM = SSD_D_INNER + 2 * SSD_BC
MLA_HEADS = 8
MLA_Q_LORA = 512
MLA_KV_LORA = 256
MLA_NOPE = 128
MLA_ROPE = 64
MLA_V = 128
MLA_QK = MLA_NOPE + MLA_ROPE
MLA_V_AUG = MLA_V + 16
MLA_QK_PAD = 256
ROPE_THETA = 10000.0
EPS = 1e-6

LANES = 128
VMEM_LIMIT_BYTES = 56 * 1024 * 1024

TOKEN_TILE = 512
IN_PROJ_TILE = 512
SSD_STEP = 512
ATTN_BLOCK = 512
FLASH_Q_BLOCK = 1024
CONV_HALO = 8

NT_DIMS = (((1,), (1,)), ((), ()))
TN_DIMS = (((0,), (0,)), ((), ()))


def _params(n_grid_axes):
    return pltpu.CompilerParams(
        dimension_semantics=("arbitrary",) * n_grid_axes,
        vmem_limit_bytes=VMEM_LIMIT_BYTES,
    )


def _resident(x, layer=None):
    if layer is None:
        zeros = (0,) * x.ndim
        return pl.BlockSpec(x.shape, lambda *_: zeros, pipeline_mode=pl.Buffered(1))
    index = (layer,) + (0,) * (x.ndim - 1)
    return pl.BlockSpec((None,) + x.shape[1:], lambda *_: index, pipeline_mode=pl.Buffered(1))


def _rms(x, gain):
    return x * lax.rsqrt(jnp.mean(x * x, axis=-1, keepdims=True) + EPS) * gain


def _silu(x):
    return x * jax.nn.sigmoid(x)


def _ffn_kernel(h_ref, g_ref, w13_ref, w2_ref, o_ref):
    x = h_ref[...]
    xb = _rms(x, g_ref[...]).astype(BF16)
    gate = jnp.dot(xb, w13_ref[:, :D_FF], preferred_element_type=F32)
    up = jnp.dot(xb, w13_ref[:, D_FF:], preferred_element_type=F32)
    hidden = (_silu(gate) * up).astype(BF16)
    o_ref[...] = x + 0.5 * jnp.dot(hidden, w2_ref[...], preferred_element_type=F32)


def _ffn(h, gain, w13, w2, layer):
    t = h.shape[0]
    return pl.pallas_call(
        _ffn_kernel,
        grid=(t // TOKEN_TILE,),
        in_specs=[
            pl.BlockSpec((TOKEN_TILE, D_MODEL), lambda i: (i, 0)),
            _resident(gain, layer),
            _resident(w13, layer),
            _resident(w2, layer),
        ],
        out_specs=pl.BlockSpec((TOKEN_TILE, D_MODEL), lambda i: (i, 0)),
        out_shape=jax.ShapeDtypeStruct((t, D_MODEL), F32),
        compiler_params=_params(1),
        name="ffn",
    )(h, gain, w13, w2)


def _in_proj_kernel(h_ref, g_ref, wz_ref, wxbc_ref, wdtkr_ref, wdtT_ref, wcq_ref, wckv_ref, wg_ref,
                    zs_ref, xbc_ref, dt_ref, dtT_ref, cq_ref, ckv_ref, krx_ref, gate_ref):
    xb = _rms(h_ref[...], g_ref[...]).astype(BF16)

    def proj(w_ref):
        return jnp.dot(xb, w_ref[...], preferred_element_type=F32)

    zs_ref[...] = _silu(proj(wz_ref)).astype(BF16)
    gate_ref[...] = jax.nn.sigmoid(proj(wg_ref)).astype(BF16)

    xbc_ref[...] = proj(wxbc_ref).astype(BF16)
    dt_krx = proj(wdtkr_ref)
    dt_ref[...] = dt_krx[:, :LANES]
    krx_ref[...] = dt_krx[:, LANES:]
    dtT_ref[...] = lax.dot_general(wdtT_ref[...], xb, NT_DIMS, preferred_element_type=F32)
    cq_ref[...] = proj(wcq_ref).astype(BF16)
    ckv_ref[...] = proj(wckv_ref).astype(BF16)


def _in_proj(h, w, layer, bsz, seq):
    tm = IN_PROJ_TILE
    nt = seq // tm
    widths = (SSD_D_INNER, SSD_CONV_DIM, LANES, None, MLA_Q_LORA, MLA_KV_LORA, LANES, 2 * D_MODEL)
    dtypes = (BF16, BF16, F32, F32, BF16, BF16, F32, BF16)
    out_specs, out_shape = [], []
    for width, dtype in zip(widths, dtypes):
        if width is None:
            out_specs.append(pl.BlockSpec((None, SSD_HEADS, tm), lambda b, i: (b, 0, i)))
            out_shape.append(jax.ShapeDtypeStruct((bsz, SSD_HEADS, seq), dtype))
        else:
            out_specs.append(pl.BlockSpec((None, tm, width), lambda b, i: (b, i, 0)))
            out_shape.append(jax.ShapeDtypeStruct((bsz, seq, width), dtype))
    weights = (w["w_z"], w["w_xbc"], w["w_dt4_krx"], w["w_dtT"], w["w_cq"], w["w_ckv"], w["w_gate"])
    return pl.pallas_call(
        _in_proj_kernel,
        grid=(bsz, nt),
        in_specs=[pl.BlockSpec((None, tm, D_MODEL), lambda b, i: (b, i, 0)), _resident(w["g_mix"], layer)]
        + [_resident(x, layer) for x in weights],
        out_specs=out_specs,
        out_shape=out_shape,
        compiler_params=_params(2),
        name="in_proj",
    )(h.reshape(bsz, seq, D_MODEL), w["g_mix"], *weights)


def _mla_prep_kernel(cq_ref, ckv_ref, krx_ref, pos_ref, inv_ref, gql_ref, wuqT_ref, gkvl_ref, wuk_ref, wuvT_ref,
                     gq_ref, gkn_ref, gkr_ref, qT_ref, k_ref, vT_ref):
    tm = cq_ref.shape[0]
    half = MLA_ROPE // 2
    ang = inv_ref[...] * pos_ref[...].astype(F32)
    cos, sin = jnp.cos(ang), jnp.sin(ang)

    qn = _rms(cq_ref[...].astype(F32), gql_ref[...]).astype(BF16)
    qT = lax.dot_general(wuqT_ref[...], qn, NT_DIMS, preferred_element_type=F32)
    gq = jnp.broadcast_to(gq_ref[...], (MLA_QK, tm))
    for h in range(MLA_HEADS):
        blk = qT[h * MLA_QK:(h + 1) * MLA_QK]
        inv_rms = lax.rsqrt(jnp.sum(blk * blk, axis=0, keepdims=True) * (1.0 / MLA_QK) + EPS)
        xn = blk * inv_rms * gq
        t1 = xn[MLA_NOPE:MLA_NOPE + half]
        t2 = xn[MLA_NOPE + half:]
        qT_ref[h, 0:MLA_NOPE, :] = xn[:MLA_NOPE].astype(BF16)
        qT_ref[h, MLA_NOPE:MLA_NOPE + half, :] = (t1 * cos - t2 * sin).astype(BF16)
        qT_ref[h, MLA_NOPE + half:MLA_QK, :] = (t2 * cos + t1 * sin).astype(BF16)
        qT_ref[h, MLA_QK:, :] = jnp.zeros((MLA_QK_PAD - MLA_QK, tm), BF16)

    kvn = _rms(ckv_ref[...].astype(F32), gkvl_ref[...]).astype(BF16)
    kn = jnp.dot(kvn, wuk_ref[...], preferred_element_type=F32)
    vT = lax.dot_general(wuvT_ref[...], kvn, NT_DIMS, preferred_element_type=F32)
    vT_ref[:, 0:MLA_V, :] = vT.reshape(MLA_HEADS, MLA_V, tm).astype(BF16)
    ones_row = (lax.broadcasted_iota(jnp.int32, (MLA_HEADS, MLA_V_AUG - MLA_V, tm), 1) == 0).astype(BF16)
    vT_ref[:, MLA_V:, :] = ones_row

    krx = krx_ref[...]
    rope_ss = 0.5 * jnp.sum(krx * krx, axis=-1, keepdims=True)
    table = (jnp.concatenate([cos, cos, sin, sin], axis=0) * gkr_ref[...]).T
    prod = krx * table
    k_rope = prod + pltpu.roll(prod, MLA_ROPE, 1)
    gkn = gkn_ref[...]
    for h in range(MLA_HEADS):
        knh = kn[:, h * MLA_NOPE:(h + 1) * MLA_NOPE]
        ss = jnp.sum(knh * knh, axis=-1, keepdims=True) + rope_ss
        inv_rms = lax.rsqrt(ss * (1.0 / MLA_QK) + EPS)
        k_ref[h, :, 0:MLA_NOPE] = (knh * inv_rms * gkn).astype(BF16)
        k_ref[h, :, MLA_NOPE:] = (k_rope * inv_rms).astype(BF16)


def _mla_prep(cq, ckv, krx, pos_row, w, layer, bsz, seq):
    tm = ATTN_BLOCK
    nt = seq // tm
    per_q = FLASH_Q_BLOCK // tm
    consts = (w["g_q_lora"], w["w_uqT"], w["g_kv_lora"], w["w_uk"], w["w_uvT"],
              w["g_q_col"], w["g_k_nope"], w["g_k_rope_col"])
    in_specs = [
        pl.BlockSpec((None, tm, MLA_Q_LORA), lambda b, i: (b, i, 0)),
        pl.BlockSpec((None, tm, MLA_KV_LORA), lambda b, i: (b, i, 0)),
        pl.BlockSpec((None, tm, LANES), lambda b, i: (b, i, 0)),
        pl.BlockSpec((None, 1, tm), lambda b, i: (b, 0, i)),
        _resident(w["rope_inv"]),
    ] + [_resident(x, layer) for x in consts]
    out_specs = [
        pl.BlockSpec((None, MLA_HEADS, None, MLA_QK_PAD, tm), lambda b, i: (b, 0, i // per_q, 0, i % per_q)),
        pl.BlockSpec((None, MLA_HEADS, tm, MLA_QK_PAD), lambda b, i: (b, 0, i, 0)),
        pl.BlockSpec((None, MLA_HEADS, None, MLA_V_AUG, tm), lambda b, i: (b, 0, i, 0, 0)),
    ]
    out_shape = [
        jax.ShapeDtypeStruct((bsz, MLA_HEADS, seq // FLASH_Q_BLOCK, MLA_QK_PAD, FLASH_Q_BLOCK), BF16),
        jax.ShapeDtypeStruct((bsz, MLA_HEADS, seq, MLA_QK_PAD), BF16),
        jax.ShapeDtypeStruct((bsz, MLA_HEADS, nt, MLA_V_AUG, tm), BF16),
    ]
    return pl.pallas_call(
        _mla_prep_kernel,
        grid=(bsz, nt),
        in_specs=in_specs,
        out_specs=out_specs,
        out_shape=out_shape,
        compiler_params=_params(2),
        name="mla_prep",
    )(cq, ckv, krx, pos_row, w["rope_inv"], *consts)


def _flash_kernel(qT_ref, k_ref, vT_ref, o_ref, s_ref, cmax_ref, m_ref, acc_ref):
    bq, bk = FLASH_Q_BLOCK, ATTN_BLOCK
    kpq = bq // bk
    nq = qT_ref.shape[0]

    def scores(qi, j, slot):
        k_blk = k_ref[pl.ds(pl.multiple_of(j * bk, bk), bk), :]
        sT = jnp.dot(k_blk, qT_ref[qi], preferred_element_type=F32)
        s_ref[slot] = sT
        cmax_ref[slot] = jnp.max(sT, axis=0, keepdims=True)

    def softmax_pv(j, sT, cmax, qs):
        m_prev = m_ref[:, qs]
        m_new = jnp.maximum(m_prev, cmax)
        alpha = jnp.exp2(m_prev - m_new)
        p = jnp.exp2(sT - m_new)
        pv = jnp.dot(vT_ref[j], p.astype(BF16), preferred_element_type=F32)
        acc_ref[:, qs] = alpha * acc_ref[:, qs] + pv
        m_ref[:, qs] = m_new

    def consume(j, slot):
        softmax_pv(j, s_ref[slot], cmax_ref[slot], slice(None))

    def query_block(i, carry):
        m_ref[...] = jnp.full(m_ref.shape, -jnp.inf, F32)
        acc_ref[...] = jnp.zeros(acc_ref.shape, F32)

        def two_blocks(j):
            scores(i, j + 1, 1)
            consume(j, 0)
            scores(i, j + 2, 0)
            consume(j + 1, 1)

        def four_blocks(t, c):
            two_blocks(4 * t)
            two_blocks(4 * t + 2)
            return c

        fours = lax.shift_right_logical(i, 1)
        lax.fori_loop(0, fours, four_blocks, 0)

        @pl.when((i & 1) == 1)
        def _():
            two_blocks(4 * fours)

        d0 = i * kpq
        upper = slice(bk, bq)
        k_hi = k_ref[pl.ds(pl.multiple_of((d0 + 1) * bk, bk), bk), :]
        s_hi = jnp.dot(k_hi, qT_ref[i, :, upper], preferred_element_type=F32)
        kpos = lax.broadcasted_iota(jnp.int32, (bk, bk), 0)
        qpos = lax.broadcasted_iota(jnp.int32, (bk, bk), 1)
        tri = kpos <= qpos
        s_lo = s_ref[0]
        s_lo = jnp.concatenate([jnp.where(tri, s_lo[:, :bk], -jnp.inf), s_lo[:, upper]], axis=1)
        softmax_pv(d0, s_lo, jnp.max(s_lo, axis=0, keepdims=True), slice(None))
        scores(jnp.minimum(i + 1, nq - 1), 0, 0)
        s_hi = jnp.where(tri, s_hi, -jnp.inf)
        softmax_pv(d0 + 1, s_hi, jnp.max(s_hi, axis=0, keepdims=True), upper)
        out = acc_ref[0:MLA_V, :] / acc_ref[MLA_V:MLA_V + 1, :]
        o_ref[pl.ds(pl.multiple_of(i * bq, bq), bq), :] = out.T.astype(BF16)
        return carry

    scores(0, 0, 0)
    lax.fori_loop(0, nq, query_block, 0)


def _flash(qT, k, vT, bsz, seq):
    bq, bk = FLASH_Q_BLOCK, ATTN_BLOCK
    assert bq == 2 * bk and seq % bq == 0
    nq = seq // bq
    return pl.pallas_call(
        _flash_kernel,
        grid=(bsz, MLA_HEADS),
        in_specs=[
            pl.BlockSpec((None, None, nq, MLA_QK_PAD, bq), lambda b, h: (b, h, 0, 0, 0)),
            pl.BlockSpec((None, None, seq, MLA_QK_PAD), lambda b, h: (b, h, 0, 0)),
            pl.BlockSpec((None, None, seq // bk, MLA_V_AUG, bk), lambda b, h: (b, h, 0, 0, 0)),
        ],
        out_specs=pl.BlockSpec((None, seq, MLA_V), lambda b, h: (b, 0, h)),
        out_shape=jax.ShapeDtypeStruct((bsz, seq, MLA_HEADS * MLA_V), BF16),
        scratch_shapes=[
            pltpu.VMEM((2, bk, bq), F32),
            pltpu.VMEM((2, 1, bq), F32),
            pltpu.VMEM((1, bq), F32),
            pltpu.VMEM((MLA_V_AUG, bq), F32),
        ],
        compiler_params=_params(2),
        name="flash",
    )(qT, k, vT)


def _softplus(x):
    return jnp.maximum(x, 0.0) + jnp.log1p(jnp.exp(-jnp.abs(x)))


def _ssd_kernel(xbc_ref, dt_ref, dtT_ref, shift_ref, e4_ref, convw_ref, convb_ref, dtb_ref, dtbT_ref, alog_ref,
                alogT_ref, dskip_ref, y_ref, state_ref, halo_ref):
    step_len = xbc_ref.shape[0]
    cl = SSD_CHUNK

    @pl.when(pl.program_id(1) == 0)
    def _():
        state_ref[...] = jnp.zeros(state_ref.shape, F32)
        halo_ref[...] = jnp.zeros(halo_ref.shape, F32)

    def conv_silu(r0, tail):
        taps = SSD_CONV - 1
        xh = xbc_ref[r0:r0 + cl, :]
        xf = xh.astype(F32)
        delayed = jnp.dot(shift_ref[...], xh, preferred_element_type=F32)
        delayed = delayed.reshape(cl // CONV_HALO, taps, CONV_HALO, SSD_CONV_DIM)
        conv = convb_ref[...] + convw_ref[taps:SSD_CONV, :] * xf
        for k in range(taps):
            conv = conv + convw_ref[k:k + 1, :] * delayed[:, k].reshape(cl, SSD_CONV_DIM)
        head_in = jnp.concatenate([tail, xf[:CONV_HALO]], axis=0)
        head = convb_ref[...] + convw_ref[taps:SSD_CONV, :] * xf[:CONV_HALO]
        for k in range(taps):
            start = CONV_HALO - taps + k
            head = head + convw_ref[k:k + 1, :] * head_in[start:start + CONV_HALO]
        return _silu(jnp.concatenate([head, conv[CONV_HALO:]], axis=0)), xf[cl - CONV_HALO:]

    tail = halo_ref[...]
    xcs = []
    for c in range(step_len // cl):
        xc_c, tail = conv_silu(c * cl, tail)
        xcs.append(xc_c)
    halo_ref[...] = tail

    dt_all = _softplus(dt_ref[...] + dtb_ref[...])
    dtT_all = _softplus(dtT_ref[...] + dtbT_ref[...])
    a_row = -jnp.exp(alog_ref[...])
    a_col = -jnp.exp(alogT_ref[...])

    row = lax.broadcasted_iota(jnp.int32, (cl, cl), 0)
    col = lax.broadcasted_iota(jnp.int32, (cl, cl), 1)
    causal = col <= row
    tril = causal.astype(F32)
    triu = (row <= col).astype(F32)
    lane = lax.broadcasted_iota(jnp.int32, (cl, LANES), 1)
    left_head = lane < SSD_HEAD_DIM
    odd_copy = (lane & SSD_HEADS) != 0

    n_chunks = step_len // cl
    acums, acumTs, splits = [], [], []
    for c in range(n_chunks):
        r0 = c * cl
        dt = dt_all[r0:r0 + cl]
        dtT = dtT_all[:, r0:r0 + cl]
        acum = jnp.dot(tril, dt * a_row, preferred_element_type=F32, precision=lax.Precision.HIGHEST)
        acumT = jnp.dot(dtT * a_col, triu, preferred_element_type=F32, precision=lax.Precision.HIGHEST)
        last = acum[cl - 1:cl, :]
        fac = jnp.where(odd_copy, dt * jnp.exp(last - acum), jnp.exp(acum))
        hi = fac.astype(BF16).astype(F32)
        splits.append(jnp.where(lane < 2 * SSD_HEADS, hi, fac - hi).astype(BF16))
        acums.append(acum)
        acumTs.append(acumT - jnp.log(dtT))
    split_all = jnp.concatenate(splits, axis=0)

    gw = SSD_HPG * SSD_HEAD_DIM
    for g in range(SSD_GROUPS):
        gsl = slice(g * gw, (g + 1) * gw)
        spread_all = jnp.dot(split_all, e4_ref[g], preferred_element_type=F32)
        for c in range(n_chunks):
            r0 = c * cl
            xc, acum, acumT_dt = xcs[c], acums[c], acumTs[c]
            bg = xc[:, SSD_D_INNER + g * SSD_STATE:SSD_D_INNER + (g + 1) * SSD_STATE].astype(BF16)
            cg = xc[:, SSD_D_INNER + SSD_BC + g * SSD_STATE:
                    SSD_D_INNER + SSD_BC + (g + 1) * SSD_STATE].astype(BF16)
            xs = xc[:, gsl]
            xs_b = xs.astype(BF16)
            spread = spread_all[r0:r0 + cl]
            y_off_scale = spread[:, :gw]
            xw = (xs * spread[:, gw:]).astype(BF16)
            state_decay = y_off_scale[cl - 1:cl, :]
            cb = lax.dot_general(cg, bg, NT_DIMS, preferred_element_type=F32)
            st = state_ref[:, gsl]
            y_off = jnp.dot(cg, st.astype(BF16), preferred_element_type=F32) * y_off_scale
            for j in range(SSD_HPG // 2):
                mats = []
                for h in (g * SSD_HPG + 2 * j, g * SSD_HPG + 2 * j + 1):
                    seg = acum[:, h:h + 1] - acumT_dt[h:h + 1, :]
                    mats.append(cb * jnp.exp(jnp.where(causal, seg, -jnp.inf)))
                lhs = jnp.concatenate(mats, axis=1).astype(BF16)
                lsl = slice(2 * j * SSD_HEAD_DIM, (2 * j + 2) * SSD_HEAD_DIM)
                xp = xs_b[:, lsl]
                zero = jnp.zeros_like(xp)
                rhs = jnp.concatenate([jnp.where(left_head, xp, zero), jnp.where(left_head, zero, xp)],
                                      axis=0)
                y_diag = jnp.dot(lhs, rhs, preferred_element_type=F32)
                csl = slice(g * gw + 2 * j * SSD_HEAD_DIM, g * gw + (2 * j + 2) * SSD_HEAD_DIM)
                y = y_diag + y_off[:, lsl] + xs[:, lsl] * dskip_ref[:, csl]
                y_ref[r0:r0 + cl, csl] = y.astype(BF16)
            state_ref[:, gsl] = st * state_decay + lax.dot_general(bg, xw, TN_DIMS, preferred_element_type=F32)


def _ssd(xbc, dt, dtT, w, layer, bsz, seq):
    ls = SSD_STEP
    shared = (w["conv_shift"], w["head_spread"])
    consts = (w["conv_w"], w["conv_b"], w["dt_bias4"], w["dt_bias_col"], w["a_log4"], w["a_log_col"], w["d_skip_x"])
    return pl.pallas_call(
        _ssd_kernel,
        grid=(bsz, seq // ls),
        in_specs=[
            pl.BlockSpec((None, ls, SSD_CONV_DIM), lambda b, i: (b, i, 0)),
            pl.BlockSpec((None, ls, LANES), lambda b, i: (b, i, 0)),
            pl.BlockSpec((None, SSD_HEADS, ls), lambda b, i: (b, 0, i)),
        ] + [_resident(x) for x in shared] + [_resident(x, layer) for x in consts],
        out_specs=pl.BlockSpec((None, ls, SSD_D_INNER), lambda b, i: (b, i, 0)),
        out_shape=jax.ShapeDtypeStruct((bsz, seq, SSD_D_INNER), BF16),
        scratch_shapes=[pltpu.VMEM((SSD_STATE, SSD_D_INNER), F32), pltpu.VMEM((CONV_HALO, SSD_CONV_DIM), F32)],
        compiler_params=_params(2),
        name="ssd",
    )(xbc, dt, dtT, *shared, *consts)


def _merge_kernel(h_ref, y_ref, zs_ref, o_ref, gate_ref, gssd_ref, wssd_ref, wmla_ref, wo_ref, out_ref):
    yg = y_ref[...].astype(F32) * zs_ref[...].astype(F32)
    y_ssd = jnp.dot(_rms(yg, gssd_ref[...]).astype(BF16), wssd_ref[...], preferred_element_type=F32)
    y_mla = jnp.dot(o_ref[...], wmla_ref[...], preferred_element_type=F32)
    gate = gate_ref[...].astype(F32)
    merged = gate[:, :D_MODEL] * y_ssd + gate[:, D_MODEL:] * y_mla
    out_ref[...] = h_ref[...] + jnp.dot(merged.astype(BF16), wo_ref[...], preferred_element_type=F32)


def _merge(h, y, zs, o, gate, w, layer):
    t = h.shape[0]
    tm = TOKEN_TILE
    consts = (w["g_ssd"], w["w_ssd_out"], w["w_mla_out"], w["w_o"])

    def rows(width):
        return pl.BlockSpec((tm, width), lambda i: (i, 0))

    return pl.pallas_call(
        _merge_kernel,
        grid=(t // tm,),
        in_specs=[rows(D_MODEL), rows(SSD_D_INNER), rows(SSD_D_INNER), rows(MLA_HEADS * MLA_V), rows(2 * D_MODEL)]
        + [_resident(x, layer) for x in consts],
        out_specs=rows(D_MODEL),
        out_shape=jax.ShapeDtypeStruct((t, D_MODEL), F32),
        compiler_params=_params(1),
        name="merge",
    )(h, y, zs, o, gate, *consts)


def _split_cols(t, sizes):
    out, start = [], 0
    for n in sizes:
        out.append(t[..., start:start + n])
        start += n
    return out


def _conv_shift_matrix():
    taps = SSD_CONV - 1
    out_row = np.arange(taps * SSD_CHUNK)
    group, rest = np.divmod(out_row, taps * CONV_HALO)
    k, r = np.divmod(rest, CONV_HALO)
    src = group * CONV_HALO + r - (taps - k)
    return jnp.asarray(src[:, None] == np.arange(SSD_CHUNK)[None, :], dtype=BF16)


def _head_spread_matrix():
    gw = SSD_HPG * SSD_HEAD_DIM
    src = np.arange(LANES)
    dst = np.arange(2 * gw)
    slabs = []
    for g in range(SSD_GROUPS):
        same_head = (src[:, None] % SSD_HEADS) == g * SSD_HPG + (dst[None, :] % gw) // SSD_HEAD_DIM
        same_half = ((src[:, None] // SSD_HEADS) % 2) == dst[None, :] // gw
        slabs.append(same_head & same_half)
    return jnp.asarray(np.stack(slabs), dtype=BF16)


def _prepare_weights(p):
    half = MLA_ROPE // 2
    depth = p["w_in"].shape[0]
    bf16 = lambda t: t.astype(BF16)
    row = lambda t: t[:, None, :]
    col = lambda t: t[:, :, None]
    transpose = lambda t: jnp.swapaxes(t, -1, -2)
    w_z, w_xbc, w_dt, w_cq, w_ckv, w_kr, w_gate = _split_cols(
        p["w_in"], (SSD_D_INNER, SSD_CONV_DIM, SSD_HEADS, MLA_Q_LORA, MLA_KV_LORA, MLA_ROPE, 2 * D_MODEL))
    w_krx = jnp.concatenate([w_kr, w_kr[..., half:], w_kr[..., :half]], axis=-1)
    w_ukv = p["w_ukv"].reshape(depth, MLA_KV_LORA, MLA_HEADS, MLA_NOPE + MLA_V)
    w_uk = w_ukv[..., :MLA_NOPE].reshape(depth, MLA_KV_LORA, MLA_HEADS * MLA_NOPE)
    w_uv = w_ukv[..., MLA_NOPE:].reshape(depth, MLA_KV_LORA, MLA_HEADS * MLA_V)
    g_q = p["q_norm"] * float(MLA_QK ** -0.5 * np.log2(np.e))
    g_k = p["k_norm"]
    g1, g2 = g_k[:, MLA_NOPE:MLA_NOPE + half], g_k[:, MLA_NOPE + half:]
    rope_inv = 1.0 / (ROPE_THETA ** (jnp.arange(0, MLA_ROPE, 2, dtype=F32) / MLA_ROPE))
    copies = LANES // SSD_HEADS
    return {
        "g_ffn1": row(p["ln_ffn1"]),
        "ffn1_w13": bf16(p["ffn1_w13"]),
        "ffn1_w2": bf16(p["ffn1_w2"]),
        "g_ffn2": row(p["ln_ffn2"]),
        "ffn2_w13": bf16(p["ffn2_w13"]),
        "ffn2_w2": bf16(p["ffn2_w2"]),
        "g_mix": row(p["ln_mix"]),
        "w_z": bf16(w_z),
        "w_xbc": bf16(w_xbc),
        "w_dt4_krx": bf16(jnp.concatenate([jnp.tile(w_dt, (1, 1, copies)), w_krx], axis=-1)),
        "w_dtT": bf16(transpose(w_dt)),
        "w_cq": bf16(w_cq),
        "w_ckv": bf16(w_ckv),
        "w_gate": bf16(w_gate),
        "conv_w": p["conv_w"],
        "conv_b": row(p["conv_b"]),
        "dt_bias4": row(jnp.tile(p["dt_bias"], (1, copies))),
        "dt_bias_col": col(p["dt_bias"]),
        "a_log4": row(jnp.tile(p["a_log"], (1, copies))),
        "a_log_col": col(p["a_log"]),
        "d_skip_x": row(jnp.repeat(p["d_skip"], SSD_HEAD_DIM, axis=-1)),
        "g_ssd": row(p["ssd_norm"]),
        "w_ssd_out": bf16(p["w_ssd_out"]),
        "g_q_lora": row(p["q_lora_norm"]),
        "w_uqT": bf16(transpose(p["w_uq"])),
        "g_kv_lora": row(p["kv_lora_norm"]),
        "w_uk": bf16(w_uk),
        "w_uvT": bf16(transpose(w_uv)),
        "g_q_col": col(g_q),
        "g_k_nope": row(g_k[:, :MLA_NOPE]),
        "g_k_rope_col": col(jnp.concatenate([g1, g2, -g2, g1], axis=-1)),
        "w_mla_out": bf16(p["w_mla_out"]),
        "w_o": bf16(p["w_o"]),
        "head_spread": _head_spread_matrix(),
        "conv_shift": _conv_shift_matrix(),
        "rope_inv": rope_inv[:, None],
    }


def kernel(x, positions, ln_ffn1, ffn1_w13, ffn1_w2, ln_mix, w_in, conv_w, conv_b, dt_bias, a_log, d_skip,
           ssd_norm, w_ssd_out, q_lora_norm, w_uq, kv_lora_norm, w_ukv, q_norm, k_norm, w_mla_out, w_o,
           ln_ffn2, ffn2_w13, ffn2_w2):
    bsz, seq, d_model = x.shape
    assert d_model == D_MODEL and seq % FLASH_Q_BLOCK == 0 and seq % SSD_STEP == 0
    assert seq % TOKEN_TILE == 0 and seq % IN_PROJ_TILE == 0
    params = dict(ln_ffn1=ln_ffn1, ffn1_w13=ffn1_w13, ffn1_w2=ffn1_w2, ln_mix=ln_mix, w_in=w_in, conv_w=conv_w,
                  conv_b=conv_b, dt_bias=dt_bias, a_log=a_log, d_skip=d_skip, ssd_norm=ssd_norm,
                  w_ssd_out=w_ssd_out, q_lora_norm=q_lora_norm, w_uq=w_uq, kv_lora_norm=kv_lora_norm, w_ukv=w_ukv,
                  q_norm=q_norm, k_norm=k_norm, w_mla_out=w_mla_out, w_o=w_o, ln_ffn2=ln_ffn2, ffn2_w13=ffn2_w13,
                  ffn2_w2=ffn2_w2)
    tokens = bsz * seq
    pos_row = positions.reshape(bsz, 1, seq)
    h = x.reshape(tokens, D_MODEL)
    w = _prepare_weights(params)
    for layer in range(ln_ffn1.shape[0]):
        h = _ffn(h, w["g_ffn1"], w["ffn1_w13"], w["ffn1_w2"], layer)
        zs, xbc, dt, dtT, cq, ckv, krx, gate = _in_proj(h, w, layer, bsz, seq)
        y = _ssd(xbc, dt, dtT, w, layer, bsz, seq)
        qT, k, vT = _mla_prep(cq, ckv, krx, pos_row, w, layer, bsz, seq)
        o = _flash(qT, k, vT, bsz, seq)
        h = _merge(h, y.reshape(tokens, SSD_D_INNER), zs.reshape(tokens, SSD_D_INNER),
                   o.reshape(tokens, MLA_HEADS * MLA_V), gate.reshape(tokens, 2 * D_MODEL), w, layer)
        h = _ffn(h, w["g_ffn2"], w["ffn2_w13"], w["ffn2_w2"], layer)
    return h.reshape(bsz, seq, D_MODEL)
```
